```python
import jax, jax.numpy as jnp
from jax import lax
import numpy as np

D_MODEL = 2048
BATCH = 4
SEQ = 2048
DEPTH = 4

CHUNK = 64
EPS = 1e-6
CONV_W = 3
GLA_HEADS = 4
GLA_DK = D_MODEL // 2
GLA_DV = D_MODEL
GLA_DK_HEAD = GLA_DK // GLA_HEADS
GLA_DV_HEAD = GLA_DV // GLA_HEADS
GLA_GATE_RANK = 16
GLA_GATE_NORM = 16.0
GLA_IN = 2 * GLA_DK + 2 * GLA_DV + GLA_GATE_RANK
D_FF = ((8 * D_MODEL // 3 + 255) // 256) * 256
N_EXPERTS = 8
TOP_K = 2
D_FF_EXPERT = D_FF
MOE_BLOCK = 256
N_CONV_LAYERS = (DEPTH + 1) // 2
N_GLA_LAYERS = DEPTH // 2

kernel_name = "hybrid_conv_gla_moe_adaln_trunk"


def _rmsnorm(x, g):
    x32 = x.astype(jnp.float32)
    y = x32 * lax.rsqrt(jnp.mean(x32 * x32, axis=-1, keepdims=True) + EPS) * g.astype(jnp.float32)
    return y.astype(x.dtype)


def _modulate(h, shift, scale):
    return h * (1 + scale) + shift


def _swiglu(h, w13, w2):
    a, b = jnp.split(h @ w13, 2, axis=-1)
    return (jax.nn.silu(a) * b) @ w2


def _conv_mixer(h, w_in, k_conv, w_out):
    S_ = h.shape[1]
    gb, gc, u = jnp.split(h @ w_in, 3, axis=-1)
    v = gc * u
    vp = jnp.pad(v, ((0, 0), (CONV_W - 1, 0), (0, 0)))
    conv = k_conv[0] * vp[:, 0:S_]
    for i in range(1, CONV_W):
        conv = conv + k_conv[i] * vp[:, i:i + S_]
    return (gb * conv) @ w_out


def _gla_mixer(h, w_in, w_gk, b_gk, norm_g, w_out):
    B_, S_, _ = h.shape
    NC = S_ // CHUNK
    proj = h @ w_in
    q, k, v, g, gk_low = jnp.split(
        proj, [GLA_DK, 2 * GLA_DK, 2 * GLA_DK + GLA_DV, 2 * GLA_DK + 2 * GLA_DV], axis=-1)
    log_a = jax.nn.log_sigmoid((gk_low @ w_gk + b_gk).astype(jnp.float32)) / GLA_GATE_NORM

    def to_chunks(t, dh):
        return t.astype(jnp.float32).reshape(B_, NC, CHUNK, GLA_HEADS, dh).transpose(1, 0, 3, 2, 4)

    qc = to_chunks(q, GLA_DK_HEAD) * (GLA_DK_HEAD ** -0.5)
    kc = to_chunks(k, GLA_DK_HEAD)
    vc = to_chunks(v, GLA_DV_HEAD)
    bc = jnp.cumsum(to_chunks(log_a, GLA_DK_HEAD), axis=3)

    def step(state, inp):
        q_, k_, v_, b_ = inp
        b_tot = b_[:, :, -1:, :]
        k_dec = k_ * jnp.exp(b_tot - b_)
        state = jnp.exp(b_tot[:, :, 0, :, None]) * state + jnp.einsum('bhck,bhcv->bhkv', k_dec, v_)
        o_ = jnp.einsum('bhck,bhkv->bhcv', q_, state)
        return state, o_

    s0 = jnp.zeros((B_, GLA_HEADS, GLA_DK_HEAD, GLA_DV_HEAD), jnp.float32)
    _, o = lax.scan(step, s0, (qc, kc, vc, bc))
    o = o.transpose(1, 0, 3, 2, 4).reshape(B_, S_, GLA_HEADS, GLA_DV_HEAD)
    o = o * lax.rsqrt(jnp.mean(o * o, axis=-1, keepdims=True) + EPS) * norm_g.astype(jnp.float32)
    o = o * jax.nn.silu(g.astype(jnp.float32)).reshape(B_, S_, GLA_HEADS, GLA_DV_HEAD)
    return o.reshape(B_, S_, GLA_DV).astype(h.dtype) @ w_out


def _moe(h, router, w13_all, w2_all, j):
    T, D = h.shape
    logits = (h @ router).astype(jnp.float32)
    top_v, top_i = lax.top_k(logits, TOP_K)
    gates = jax.nn.softmax(top_v, axis=-1)
    TK = T * TOP_K
    flat_e = top_i.reshape(TK)
    flat_tok = jnp.arange(TK, dtype=jnp.int32) // TOP_K
    flat_gate = gates.reshape(TK)
    order = jnp.argsort(flat_e, stable=True)
    sorted_e = flat_e[order]
    counts = jnp.bincount(flat_e, length=N_EXPERTS)
    padded = (counts + MOE_BLOCK - 1) // MOE_BLOCK * MOE_BLOCK
    pad_end = jnp.cumsum(padded)
    pad_start = pad_end - padded
    start = jnp.cumsum(counts) - counts
    dest = pad_start[sorted_e] + jnp.arange(TK, dtype=jnp.int32) - start[sorted_e]
    n_blocks = -(-TK // MOE_BLOCK) + N_EXPERTS
    n_slots = n_blocks * MOE_BLOCK
    slot_tok = jnp.full((n_slots,), T, jnp.int32).at[dest].set(flat_tok[order])
    slot_gate = jnp.zeros((n_slots,), jnp.float32).at[dest].set(flat_gate[order])
    block_start = jnp.arange(n_blocks, dtype=jnp.int32) * MOE_BLOCK
    block_e = jnp.minimum(jnp.sum(block_start[:, None] >= pad_end[None, :], axis=1), N_EXPERTS - 1)
    h_pad = jnp.concatenate([h, jnp.zeros((1, D), h.dtype)], axis=0)

    def expert_block(args):
        tok, e = args
        return _swiglu(h_pad[tok], w13_all[j, e], w2_all[j, e])

    y_slots = lax.map(expert_block, (slot_tok.reshape(n_blocks, MOE_BLOCK), block_e))
    y_slots = y_slots.reshape(n_slots, D) * slot_gate[:, None].astype(h.dtype)
    return jax.ops.segment_sum(y_slots, slot_tok, num_segments=T + 1)[:T]


def setup_inputs(seed: int = 0) -> dict:
    key = jax.random.key(seed)
    ks = jax.random.split(key, 20)
    D = D_MODEL
    f32 = jnp.float32

    def nrm(k, shape, scale):
        return jax.random.normal(k, shape, f32) * scale

    return {
        "x": nrm(ks[0], (BATCH, SEQ, D), 1.0),
        "c": nrm(ks[1], (BATCH, D), 1.0),
        "ada_w": nrm(ks[2], (DEPTH, D, 6 * D), 0.5 * D ** -0.5),
        "ada_b": nrm(ks[3], (DEPTH, 6 * D), 0.01),
        "norm_g": 1.0 + nrm(ks[4], (DEPTH, 2, D), 0.02),
        "conv_w_in": nrm(ks[5], (N_CONV_LAYERS, D, 3 * D), D ** -0.5),
        "conv_k": nrm(ks[6], (N_CONV_LAYERS, CONV_W, D), CONV_W ** -0.5),
        "conv_w_out": nrm(ks[7], (N_CONV_LAYERS, D, D), D ** -0.5),
        "gla_w_in": nrm(ks[8], (N_GLA_LAYERS, D, GLA_IN), D ** -0.5),
        "gla_w_gk": nrm(ks[9], (N_GLA_LAYERS, GLA_GATE_RANK, GLA_DK), GLA_GATE_RANK ** -0.5),
        "gla_b_gk": 2.0 + nrm(ks[10], (N_GLA_LAYERS, GLA_DK), 0.1),
        "gla_norm_g": 1.0 + nrm(ks[11], (N_GLA_LAYERS, GLA_DV_HEAD), 0.02),
        "gla_w_out": nrm(ks[12], (N_GLA_LAYERS, GLA_DV, D), GLA_DV ** -0.5),
        "ffn_w13": nrm(ks[13], (N_CONV_LAYERS, D, 2 * D_FF), D ** -0.5),
        "ffn_w2": nrm(ks[14], (N_CONV_LAYERS, D_FF, D), D_FF ** -0.5),
        "moe_router": nrm(ks[15], (N_GLA_LAYERS, D, N_EXPERTS), D ** -0.5),
        "moe_w13": nrm(ks[16], (N_GLA_LAYERS, N_EXPERTS, D, 2 * D_FF_EXPERT), D ** -0.5),
        "moe_w2": nrm(ks[17], (N_GLA_LAYERS, N_EXPERTS, D_FF_EXPERT, D), D_FF_EXPERT ** -0.5),
        "final_g": 1.0 + nrm(ks[18], (D,), 0.02),
    }


def reference(x, c, ada_w, ada_b, norm_g, conv_w_in, conv_k, conv_w_out,
              gla_w_in, gla_w_gk, gla_b_gk, gla_norm_g, gla_w_out,
              ffn_w13, ffn_w2, moe_router, moe_w13, moe_w2, final_g):
    B_, S_, D = x.shape
    c_act = jax.nn.silu(c)
    for i in range(DEPTH):
        j = i // 2
        mod = (c_act @ ada_w[i] + ada_b[i])[:, None, :]
        sh1, sc1, g1, sh2, sc2, g2 = jnp.split(mod, 6, axis=-1)
        h = _modulate(_rmsnorm(x, norm_g[i, 0]), sh1, sc1)
        if i % 2 == 0:
            m = _conv_mixer(h, conv_w_in[j], conv_k[j], conv_w_out[j])
        else:
            m = _gla_mixer(h, gla_w_in[j], gla_w_gk[j], gla_b_gk[j], gla_norm_g[j], gla_w_out[j])
        x = x + g1 * m
        h = _modulate(_rmsnorm(x, norm_g[i, 1]), sh2, sc2)
        if i % 2 == 0:
            f = _swiglu(h, ffn_w13[j], ffn_w2[j])
        else:
            f = _moe(h.reshape(B_ * S_, D), moe_router[j], moe_w13, moe_w2, j).reshape(B_, S_, D)
        x = x + g2 * f
    return _rmsnorm(x, final_g)
```

```python
import functools

import jax
import jax.numpy as jnp
from jax import lax
from jax.experimental import pallas as pl
from jax.experimental.pallas import tpu as pltpu

D_MODEL = 2048
BATCH = 4
SEQ = 2048
TOKENS = BATCH * SEQ
DEPTH = 4
CHUNK = 64
EPS = 1e-6
CONV_W = 3
GLA_HEADS = 4
GLA_DK = D_MODEL // 2
GLA_DV = D_MODEL
GLA_DK_HEAD = GLA_DK // GLA_HEADS
GLA_DV_HEAD = GLA_DV // GLA_HEADS
GLA_GATE_RANK = 16
GLA_GATE_NORM = 16.0
GLA_MAIN = 2 * GLA_DK + 2 * GLA_DV
D_FF = 5632
N_EXPERTS = 8
TOP_K = 2

LANES = 128
SUBLANES = 8
VMEM_LIMIT = 56 * 1024 * 1024

MOE_ROWS = 256
N_SLOT_BLOCKS = TOKENS * TOP_K // MOE_ROWS + N_EXPERTS
N_SLOTS = N_SLOT_BLOCKS * MOE_ROWS

F32 = jnp.float32
BF16 = jnp.bfloat16


def _params(n_axes):
    return pltpu.CompilerParams(
        dimension_semantics=("arbitrary",) * n_axes, vmem_limit_bytes=VMEM_LIMIT)


def _dot(a, b):
    return jnp.dot(a, b, preferred_element_type=F32)


def _silu(v):
    return v * jax.nn.sigmoid(v)


ADA_TN = 1024


def _ada_kernel(c_ref, w_ref, b_ref, o_ref):
    c_act = _silu(c_ref[...]).astype(BF16)
    o_ref[...] = _dot(c_act, w_ref[...].astype(BF16)) + b_ref[...]


def _ada_all(c, ada_w, ada_b):
    c_pad = jnp.pad(c, ((0, SUBLANES - BATCH), (0, 0)))
    n = 6 * D_MODEL
    return pl.pallas_call(
        _ada_kernel,
        grid=(DEPTH, n // ADA_TN),
        in_specs=[
            pl.BlockSpec((SUBLANES, D_MODEL), lambda l, j: (0, 0)),
            pl.BlockSpec((None, D_MODEL, ADA_TN), lambda l, j: (l, 0, j)),
            pl.BlockSpec((None, 1, ADA_TN), lambda l, j: (l, 0, j)),
        ],
        out_specs=pl.BlockSpec((None, SUBLANES, ADA_TN), lambda l, j: (l, 0, j)),
        out_shape=jax.ShapeDtypeStruct((DEPTH, SUBLANES, n), F32),
        compiler_params=_params(2),
        name="ada_mod",
    )(c_pad, ada_w, ada_b.reshape(DEPTH, 1, n))


NORM_TM = 512


def _norm_mod(x, g, sc, sh):
    ms = jnp.mean(x * x, axis=-1, keepdims=True)
    y = x * lax.rsqrt(ms + EPS) * g
    return y * (1.0 + sc) + sh


def _norm_mod_kernel(x_ref, g_ref, sh_ref, sc_ref, h_ref):
    h_ref[...] = _norm_mod(x_ref[...], g_ref[...], sc_ref[...], sh_ref[...]).astype(h_ref.dtype)


def _row_specs(tm):
    per_seq = SEQ // tm
    rows = pl.BlockSpec((tm, D_MODEL), lambda i: (i, 0))
    vec = pl.BlockSpec((1, D_MODEL), lambda i: (0, 0))
    seq_vec = pl.BlockSpec((None, 1, D_MODEL), lambda i: (i // per_seq, 0, 0))
    return rows, vec, seq_vec


def _norm_modulate(x, g, sh, sc):
    rows, vec, seq_vec = _row_specs(NORM_TM)
    return pl.pallas_call(
        _norm_mod_kernel,
        grid=(TOKENS // NORM_TM,),
        in_specs=[rows, vec, seq_vec, seq_vec],
        out_specs=rows,
        out_shape=jax.ShapeDtypeStruct((TOKENS, D_MODEL), BF16),
        compiler_params=_params(1),
        name="norm_mod",
    )(x, g.reshape(1, D_MODEL), sh, sc)


MM_TM = 1024
MM_TN = 512


def _cast_weights_once(w_refs, wb_refs):
    @pl.when(pl.program_id(1) == 0)
    def _():
        for w_ref, wb_ref in zip(w_refs, wb_refs):
            wb_ref[...] = w_ref[...].astype(BF16)


def _mm_kernel(a_ref, w_ref, o_ref, wb_ref):
    _cast_weights_once((w_ref,), (wb_ref,))
    o_ref[...] = _dot(a_ref[...], wb_ref[...]).astype(o_ref.dtype)


def _matmul(a, w3, layer, n_out, out_dtype):
    k = a.shape[1]
    return pl.pallas_call(
        _mm_kernel,
        grid=(n_out // MM_TN, TOKENS // MM_TM),
        in_specs=[
            pl.BlockSpec((MM_TM, k), lambda j, i: (i, 0)),
            pl.BlockSpec((None, k, MM_TN), lambda j, i: (layer, 0, j)),
        ],
        out_specs=pl.BlockSpec((MM_TM, MM_TN), lambda j, i: (i, j)),
        out_shape=jax.ShapeDtypeStruct((TOKENS, n_out), out_dtype),
        scratch_shapes=[pltpu.VMEM((k, MM_TN), BF16)],
        compiler_params=_params(2),
        name="matmul",
    )(a, w3)


RES_TM = 512


def _mm_res_kernel(a_ref, w_ref, x_ref, g_ref, o_ref, wb_ref):
    _cast_weights_once((w_ref,), (wb_ref,))
    o_ref[...] = x_ref[...] + g_ref[...] * _dot(a_ref[...], wb_ref[...])


def _matmul_residual(a, w3, layer, x, gate):
    k = a.shape[1]
    per_seq = SEQ // RES_TM
    return pl.pallas_call(
        _mm_res_kernel,
        grid=(D_MODEL // MM_TN, TOKENS // RES_TM),
        in_specs=[
            pl.BlockSpec((RES_TM, k), lambda j, i: (i, 0)),
            pl.BlockSpec((None, k, MM_TN), lambda j, i: (layer, 0, j)),
            pl.BlockSpec((RES_TM, MM_TN), lambda j, i: (i, j)),
            pl.BlockSpec((None, 1, MM_TN), lambda j, i: (i // per_seq, 0, j)),
        ],
        out_specs=pl.BlockSpec((RES_TM, MM_TN), lambda j, i: (i, j)),
        out_shape=jax.ShapeDtypeStruct((TOKENS, D_MODEL), F32),
        scratch_shapes=[pltpu.VMEM((k, MM_TN), BF16)],
        compiler_params=_params(2),
        name="matmul_residual",
    )(a, w3, x, gate)


CONV_TM = 1024


def _conv_in_kernel(a_ref, wgb_ref, wgc_ref, wu_ref, kc_ref, o_ref, wb_ref, v_ref):
    i = pl.program_id(1)
    _cast_weights_once((wgb_ref, wgc_ref, wu_ref), (wb_ref.at[0], wb_ref.at[1], wb_ref.at[2]))

    @pl.when(i % (SEQ // CONV_TM) == 0)
    def _():
        v_ref[0:SUBLANES, :] = jnp.zeros((SUBLANES, MM_TN), F32)

    a = a_ref[...]
    gc = _dot(a, wb_ref[1])
    u = _dot(a, wb_ref[2])
    v_ref[SUBLANES:SUBLANES + CONV_TM, :] = gc * u
    conv = kc_ref[2:3, :] * v_ref[SUBLANES:SUBLANES + CONV_TM, :]
    conv = conv + kc_ref[1:2, :] * v_ref[SUBLANES - 1:SUBLANES - 1 + CONV_TM, :]
    conv = conv + kc_ref[0:1, :] * v_ref[SUBLANES - 2:SUBLANES - 2 + CONV_TM, :]
    gb = _dot(a, wb_ref[0])
    o_ref[...] = (gb * conv).astype(o_ref.dtype)
    v_ref[0:SUBLANES, :] = v_ref[CONV_TM:CONV_TM + SUBLANES, :]


def _conv_in(h, conv_w_in, conv_k, layer):
    nb = D_MODEL // MM_TN
    w_spec = lambda off: pl.BlockSpec((None, D_MODEL, MM_TN), lambda j, i: (layer, 0, j + off))
    return pl.pallas_call(
        _conv_in_kernel,
        grid=(nb, TOKENS // CONV_TM),
        in_specs=[
            pl.BlockSpec((CONV_TM, D_MODEL), lambda j, i: (i, 0)),
            w_spec(0), w_spec(nb), w_spec(2 * nb),
            pl.BlockSpec((None, CONV_W, MM_TN), lambda j, i: (layer, 0, j)),
        ],
        out_specs=pl.BlockSpec((CONV_TM, MM_TN), lambda j, i: (i, j)),
        out_shape=jax.ShapeDtypeStruct((TOKENS, D_MODEL), BF16),
        scratch_shapes=[
            pltpu.VMEM((3, D_MODEL, MM_TN), BF16),
            pltpu.VMEM((CONV_TM + SUBLANES, MM_TN), F32),
        ],
        compiler_params=_params(2),
        name="conv_in",
    )(h, conv_w_in, conv_w_in, conv_w_in, conv_k)


def _ffn_up_kernel(a_ref, w1_ref, w3_ref, o_ref, wb_ref):
    _cast_weights_once((w1_ref, w3_ref), (wb_ref.at[0], wb_ref.at[1]))
    a = a_ref[...]
    p = _dot(a, wb_ref[0])
    q = _dot(a, wb_ref[1])
    o_ref[...] = (_silu(p) * q).astype(o_ref.dtype)


def _ffn_up(h, w13, layer):
    nb = D_FF // MM_TN
    w_spec = lambda off: pl.BlockSpec((None, D_MODEL, MM_TN), lambda j, i: (layer, 0, j + off))
    return pl.pallas_call(
        _ffn_up_kernel,
        grid=(nb, TOKENS // MM_TM),
        in_specs=[pl.BlockSpec((MM_TM, D_MODEL), lambda j, i: (i, 0)), w_spec(0), w_spec(nb)],
        out_specs=pl.BlockSpec((MM_TM, MM_TN), lambda j, i: (i, j)),
        out_shape=jax.ShapeDtypeStruct((TOKENS, D_FF), BF16),
        scratch_shapes=[pltpu.VMEM((2, D_MODEL, MM_TN), BF16)],
        compiler_params=_params(2),
        name="ffn_up",
    )(h, w13, w13)


LOGA_TM = 512


def _loga_kernel(h_ref, wl_ref, wgk_ref, bgk_ref, o_ref):
    low = _dot(h_ref[...], wl_ref[...].astype(BF16))
    z = _dot(low.astype(BF16), wgk_ref[...].astype(BF16)) + bgk_ref[...]
    log_sig = jnp.minimum(z, 0.0) - jnp.log1p(jnp.exp(-jnp.abs(z)))
    o_ref[...] = log_sig * (1.0 / GLA_GATE_NORM)


def _gla_log_decay(h, w_low, w_gk, b_gk):
    w_low = jnp.pad(w_low, ((0, 0), (0, LANES - GLA_GATE_RANK)))
    w_gk = jnp.pad(w_gk, ((0, LANES - GLA_GATE_RANK), (0, 0)))
    return pl.pallas_call(
        _loga_kernel,
        grid=(TOKENS // LOGA_TM,),
        in_specs=[
            pl.BlockSpec((LOGA_TM, D_MODEL), lambda i: (i, 0)),
            pl.BlockSpec((D_MODEL, LANES), lambda i: (0, 0)),
            pl.BlockSpec((LANES, GLA_DK), lambda i: (0, 0)),
            pl.BlockSpec((1, GLA_DK), lambda i: (0, 0)),
        ],
        out_specs=pl.BlockSpec((LOGA_TM, GLA_DK), lambda i: (i, 0)),
        out_shape=jax.ShapeDtypeStruct((TOKENS, GLA_DK), F32),
        compiler_params=_params(1),
        name="gla_log_decay",
    )(h, w_low, w_gk, b_gk.reshape(1, GLA_DK))


GLA_ROWS = 512


def _gla_kernel(q_ref, k_ref, v_ref, g_ref, la_ref, ng_ref, o_ref, st_ref):
    @pl.when(pl.program_id(1) == 0)
    def _():
        st_ref[...] = jnp.zeros(st_ref.shape, F32)

    row = lax.broadcasted_iota(jnp.int32, (CHUNK, CHUNK), 0)
    col = lax.broadcasted_iota(jnp.int32, (CHUNK, CHUNK), 1)
    tri = (col <= row).astype(BF16)

    def chunk(ci, carry):
        rows = pl.ds(pl.multiple_of(ci * CHUNK, CHUNK), CHUNK)
        la = la_ref[rows, :]
        la_hi = la.astype(BF16)
        la_lo = (la - la_hi.astype(F32)).astype(BF16)
        bcum = _dot(tri, la_hi) + _dot(tri, la_lo)
        btot = bcum[CHUNK - 1:CHUNK, :]
        k_dec = (k_ref[rows, :].astype(F32) * jnp.exp(btot - bcum)).astype(BF16)
        decay = jnp.exp(btot)
        q = (q_ref[rows, :].astype(F32) * (GLA_DK_HEAD ** -0.5)).astype(BF16)
        for h in range(GLA_HEADS):
            kcols = slice(h * GLA_DK_HEAD, (h + 1) * GLA_DK_HEAD)
            vcols = slice(h * GLA_DV_HEAD, (h + 1) * GLA_DV_HEAD)
            kv_t = lax.dot_general(v_ref[rows, vcols], k_dec[:, kcols],
                                   (((0,), (0,)), ((), ())), preferred_element_type=F32)
            st = st_ref[h] * decay[:, kcols] + kv_t
            st_ref[h] = st
            o = lax.dot_general(q[:, kcols], st.astype(BF16),
                                (((1,), (1,)), ((), ())), preferred_element_type=F32)
            o = o * lax.rsqrt(jnp.mean(o * o, axis=-1, keepdims=True) + EPS) * ng_ref[...]
            o = o * _silu(g_ref[rows, vcols].astype(F32))
            o_ref[rows, vcols] = o.astype(o_ref.dtype)
        return carry

    lax.fori_loop(0, GLA_ROWS // CHUNK, chunk, 0)


def _gla_scan(proj, log_a, norm_g):
    per_seq = SEQ // GLA_ROWS
    rows = lambda width, blk: pl.BlockSpec((GLA_ROWS, width), lambda b, s: (b * per_seq + s, blk))
    return pl.pallas_call(
        _gla_kernel,
        grid=(BATCH, per_seq),
        in_specs=[
            rows(GLA_DK, 0), rows(GLA_DK, 1), rows(GLA_DV, 1), rows(GLA_DV, 2),
            rows(GLA_DK, 0),
            pl.BlockSpec((1, GLA_DV_HEAD), lambda b, s: (0, 0)),
        ],
        out_specs=rows(GLA_DV, 0),
        out_shape=jax.ShapeDtypeStruct((TOKENS, GLA_DV), BF16),
        scratch_shapes=[pltpu.VMEM((GLA_HEADS, GLA_DV_HEAD, GLA_DK_HEAD), F32)],
        compiler_params=_params(2),
        name="gla_scan",
    )(proj, proj, proj, proj, log_a, norm_g.reshape(1, GLA_DV_HEAD))


ROUTE_TM = 512
META_E0, META_E1, META_G0, META_G1, META_R0, META_R1 = range(6)


def _route_kernel(x_ref, g_ref, sh_ref, sc_ref, r_ref, h_ref, meta_ref, cnt_ref, run_ref):
    @pl.when(pl.program_id(0) == 0)
    def _():
        run_ref[...] = jnp.zeros(run_ref.shape, F32)

    h = _norm_mod(x_ref[...], g_ref[...], sc_ref[...], sh_ref[...])
    h_ref[...] = h
    logits = jnp.dot(h, r_ref[...], preferred_element_type=F32, precision=lax.Precision.HIGHEST)
    lane = lax.broadcasted_iota(jnp.int32, logits.shape, 1)
    logits = jnp.where(lane < N_EXPERTS, logits, -jnp.inf)
    m0 = jnp.max(logits, axis=1, keepdims=True)
    e0 = jnp.min(jnp.where(logits == m0, lane, LANES), axis=1, keepdims=True)
    rest = jnp.where(lane == e0, -jnp.inf, logits)
    m1 = jnp.max(rest, axis=1, keepdims=True)
    e1 = jnp.min(jnp.where(rest == m1, lane, LANES), axis=1, keepdims=True)
    p = jnp.exp(m1 - m0)
    gate0 = 1.0 / (1.0 + p)
    gate1 = p / (1.0 + p)

    hot0 = (lane == e0).astype(F32)
    hot1 = (lane == e1).astype(F32)
    both = hot0 + hot1
    row = lax.broadcasted_iota(jnp.int32, (ROUTE_TM, ROUTE_TM), 0)
    col = lax.broadcasted_iota(jnp.int32, (ROUTE_TM, ROUTE_TM), 1)
    before = _dot((col < row).astype(BF16), both.astype(BF16)) + run_ref[0:1, :]
    rank0 = jnp.sum(hot0 * before, axis=1, keepdims=True)
    rank1 = jnp.sum(hot1 * before, axis=1, keepdims=True)
    run_ref[...] = run_ref[...] + jnp.sum(both, axis=0, keepdims=True)
    cnt_ref[...] = run_ref[...]

    meta = jnp.zeros(logits.shape, F32)
    for lane_id, val in ((META_E0, e0.astype(F32)), (META_E1, e1.astype(F32)), (META_G0, gate0),
                         (META_G1, gate1), (META_R0, rank0), (META_R1, rank1)):
        meta = jnp.where(lane == lane_id, val, meta)
    meta_ref[...] = meta


def _route(x, g, sh, sc, router):
    rows, vec, seq_vec = _row_specs(ROUTE_TM)
    router = jnp.pad(router, ((0, 0), (0, LANES - N_EXPERTS)))
    return pl.pallas_call(
        _route_kernel,
        grid=(TOKENS // ROUTE_TM,),
        in_specs=[rows, vec, seq_vec, seq_vec, pl.BlockSpec((D_MODEL, LANES), lambda i: (0, 0))],
        out_specs=[
            rows,
            pl.BlockSpec((ROUTE_TM, LANES), lambda i: (i, 0)),
            pl.BlockSpec((SUBLANES, LANES), lambda i: (0, 0)),
        ],
        out_shape=[
            jax.ShapeDtypeStruct((TOKENS, D_MODEL), F32),
            jax.ShapeDtypeStruct((TOKENS, LANES), F32),
            jax.ShapeDtypeStruct((SUBLANES, LANES), F32),
        ],
        scratch_shapes=[pltpu.VMEM((SUBLANES, LANES), F32)],
        compiler_params=_params(1),
        name="moe_route",
    )(x, g.reshape(1, D_MODEL), sh, sc, router)


def _row_copy(src_hbm, src_row, dst_vmem, dst_row, sem):
    return pltpu.make_async_copy(
        src_hbm.at[pl.ds(src_row, 1), :], dst_vmem.at[pl.ds(dst_row, 1), :], sem)


def _gather_kernel(tok_ref, h_hbm, o_ref, buf_ref, sem):
    base = pl.program_id(0) * MOE_ROWS

    def start(r, carry):
        _row_copy(h_hbm, tok_ref[base + r], buf_ref, r, sem).start()
        return carry

    def wait(r, carry):
        _row_copy(h_hbm, 0, buf_ref, r, sem).wait()
        return carry

    lax.fori_loop(0, MOE_ROWS, start, 0)
    lax.fori_loop(0, MOE_ROWS, wait, 0)
    o_ref[...] = buf_ref[...].astype(o_ref.dtype)


def _gather_slots(slot_tok, h):
    return pl.pallas_call(
        _gather_kernel,
        grid_spec=pltpu.PrefetchScalarGridSpec(
            num_scalar_prefetch=1,
            grid=(N_SLOT_BLOCKS,),
            in_specs=[pl.BlockSpec(memory_space=pl.ANY)],
            out_specs=pl.BlockSpec((MOE_ROWS, D_MODEL), lambda b, tok: (b, 0)),
            scratch_shapes=[pltpu.VMEM((MOE_ROWS, D_MODEL), F32), pltpu.SemaphoreType.DMA(())],
        ),
        out_shape=jax.ShapeDtypeStruct((N_SLOTS, D_MODEL), BF16),
        compiler_params=_params(1),
        name="moe_gather",
    )(slot_tok, h)


def _expert_changed(be_ref, b):
    return jnp.logical_or(b == 0, be_ref[b] != be_ref[jnp.maximum(b - 1, 0)])


def _moe_up_kernel(be_ref, nu_ref, a_ref, w1_ref, w3_ref, o_ref, wb_ref):
    b = pl.program_id(1)

    @pl.when(_expert_changed(be_ref, b))
    def _():
        wb_ref[0] = w1_ref[...].astype(BF16)
        wb_ref[1] = w3_ref[...].astype(BF16)

    @pl.when(b < nu_ref[0])
    def _():
        a = a_ref[...]
        p = _dot(a, wb_ref[0])
        q = _dot(a, wb_ref[1])
        o_ref[...] = (_silu(p) * q).astype(o_ref.dtype)

    @pl.when(b >= nu_ref[0])
    def _():
        o_ref[...] = jnp.zeros(o_ref.shape, o_ref.dtype)


def _moe_up(block_e, n_used, xs, w13, layer):
    nb = D_FF // MM_TN
    w_spec = lambda off: pl.BlockSpec(
        (None, None, D_MODEL, MM_TN), lambda f, b, be, nu: (layer, be[b], 0, f + off))
    return pl.pallas_call(
        _moe_up_kernel,
        grid_spec=pltpu.PrefetchScalarGridSpec(
            num_scalar_prefetch=2,
            grid=(nb, N_SLOT_BLOCKS),
            in_specs=[pl.BlockSpec((MOE_ROWS, D_MODEL), lambda f, b, be, nu: (b, 0)),
                      w_spec(0), w_spec(nb)],
            out_specs=pl.BlockSpec((MOE_ROWS, MM_TN), lambda f, b, be, nu: (b, f)),
            scratch_shapes=[pltpu.VMEM((2, D_MODEL, MM_TN), BF16)],
        ),
        out_shape=jax.ShapeDtypeStruct((N_SLOTS, D_FF), BF16),
        compiler_params=_params(2),
        name="moe_up",
    )(block_e, n_used, xs, w13, w13)


def _moe_down_kernel(be_ref, nu_ref, a_ref, w_ref, o_ref, wb_ref):
    b = pl.program_id(1)

    @pl.when(_expert_changed(be_ref, b))
    def _():
        wb_ref[...] = w_ref[...].astype(BF16)

    @pl.when(b < nu_ref[0])
    def _():
        o_ref[...] = _dot(a_ref[...], wb_ref[...])

    @pl.when(b >= nu_ref[0])
    def _():
        o_ref[...] = jnp.zeros(o_ref.shape, o_ref.dtype)


def _moe_down(block_e, n_used, t, w2, layer):
    return pl.pallas_call(
        _moe_down_kernel,
        grid_spec=pltpu.PrefetchScalarGridSpec(
            num_scalar_prefetch=2,
            grid=(D_MODEL // MM_TN, N_SLOT_BLOCKS),
            in_specs=[
                pl.BlockSpec((MOE_ROWS, D_FF), lambda n, b, be, nu: (b, 0)),
                pl.BlockSpec((None, None, D_FF, MM_TN), lambda n, b, be, nu: (layer, be[b], 0, n)),
            ],
            out_specs=pl.BlockSpec((MOE_ROWS, MM_TN), lambda n, b, be, nu: (b, n)),
            scratch_shapes=[pltpu.VMEM((D_FF, MM_TN), BF16)],
        ),
        out_shape=jax.ShapeDtypeStruct((N_SLOTS, D_MODEL), F32),
        compiler_params=_params(2),
        name="moe_down",
    )(block_e, n_used, t, w2)


COMB_TM = 256


def _combine_kernel(final, dest_ref, x_ref, meta_ref, g2_ref, fg_ref, y_hbm, o_ref, buf_ref, sem):
    base = pl.program_id(0) * COMB_TM

    def start(r, carry):
        for k in range(TOP_K):
            _row_copy(y_hbm, dest_ref[TOP_K * (base + r) + k], buf_ref.at[k], r, sem).start()
        return carry

    def wait(r, carry):
        for k in range(TOP_K):
            _row_copy(y_hbm, 0, buf_ref.at[k], r, sem).wait()
        return carry

    lax.fori_loop(0, COMB_TM, start, 0)
    lax.fori_loop(0, COMB_TM, wait, 0)
    meta = meta_ref[...]
    f = meta[:, META_G0:META_G0 + 1] * buf_ref[0] + meta[:, META_G1:META_G1 + 1] * buf_ref[1]
    x_new = x_ref[...] + g2_ref[...] * f
    if final:
        ms = jnp.mean(x_new * x_new, axis=-1, keepdims=True)
        x_new = x_new * lax.rsqrt(ms + EPS) * fg_ref[...]
    o_ref[...] = x_new


def _combine(dest_flat, x, meta, g2, y_slots, final_g, final):
    per_seq = SEQ // COMB_TM
    return pl.pallas_call(
        functools.partial(_combine_kernel, final),
        grid_spec=pltpu.PrefetchScalarGridSpec(
            num_scalar_prefetch=1,
            grid=(TOKENS // COMB_TM,),
            in_specs=[
                pl.BlockSpec((COMB_TM, D_MODEL), lambda i, d: (i, 0)),
                pl.BlockSpec((COMB_TM, LANES), lambda i, d: (i, 0)),
                pl.BlockSpec((None, 1, D_MODEL), lambda i, d: (i // per_seq, 0, 0)),
                pl.BlockSpec((1, D_MODEL), lambda i, d: (0, 0)),
                pl.BlockSpec(memory_space=pl.ANY),
            ],
            out_specs=pl.BlockSpec((COMB_TM, D_MODEL), lambda i, d: (i, 0)),
            scratch_shapes=[pltpu.VMEM((TOP_K, COMB_TM, D_MODEL), F32), pltpu.SemaphoreType.DMA(())],
        ),
        out_shape=jax.ShapeDtypeStruct((TOKENS, D_MODEL), F32),
        compiler_params=_params(1),
        name="moe_combine",
    )(dest_flat, x, meta, g2, final_g.reshape(1, D_MODEL), y_slots)


def _slot_plan(meta, counts):
    top_e = meta[:, META_E0:META_E1 + 1].astype(jnp.int32)
    rank = meta[:, META_R0:META_R1 + 1].astype(jnp.int32)
    counts = counts[0, :N_EXPERTS].astype(jnp.int32)
    padded = (counts + MOE_ROWS - 1) // MOE_ROWS * MOE_ROWS
    pad_end = jnp.cumsum(padded)
    pad_start = pad_end - padded
    dest = (pad_start[top_e] + rank).reshape(TOKENS * TOP_K)
    pair_tok = jnp.arange(TOKENS * TOP_K, dtype=jnp.int32) // TOP_K
    slot_tok = jnp.zeros((N_SLOTS,), jnp.int32).at[dest].set(pair_tok)
    block_start = jnp.arange(N_SLOT_BLOCKS, dtype=jnp.int32) * MOE_ROWS
    block_e = jnp.minimum(jnp.sum(block_start[:, None] >= pad_end[None, :], axis=1), N_EXPERTS - 1)
    n_used = (pad_end[N_EXPERTS - 1] // MOE_ROWS).reshape(1)
    return dest, slot_tok, block_e.astype(jnp.int32), n_used.astype(jnp.int32)


def kernel(x, c, ada_w, ada_b, norm_g, conv_w_in, conv_k, conv_w_out, gla_w_in, gla_w_gk, gla_b_gk,
           gla_norm_g, gla_w_out, ffn_w13, ffn_w2, moe_router, moe_w13, moe_w2, final_g):
    assert x.shape == (BATCH, SEQ, D_MODEL) and x.dtype == F32
    mod = _ada_all(c, ada_w, ada_b)[:, :BATCH]
    xt = x.reshape(TOKENS, D_MODEL)
    for i in range(DEPTH):
        j = i // 2
        sh1, sc1, g1, sh2, sc2, g2 = (
            mod[i, :, n * D_MODEL:(n + 1) * D_MODEL].reshape(BATCH, 1, D_MODEL) for n in range(6))
        h = _norm_modulate(xt, norm_g[i, 0], sh1, sc1)
        if i % 2 == 0:
            y = _conv_in(h, conv_w_in, conv_k, j)
            xt = _matmul_residual(y, conv_w_out, j, xt, g1)
            h = _norm_modulate(xt, norm_g[i, 1], sh2, sc2)
            t = _ffn_up(h, ffn_w13, j)
            xt = _matmul_residual(t, ffn_w2, j, xt, g2)
        else:
            proj = _matmul(h, gla_w_in, j, GLA_MAIN, BF16)
            log_a = _gla_log_decay(h, gla_w_in[j, :, GLA_MAIN:], gla_w_gk[j], gla_b_gk[j])
            o = _gla_scan(proj, log_a, gla_norm_g[j])
            xt = _matmul_residual(o, gla_w_out, j, xt, g1)
            h32, meta, counts = _route(xt, norm_g[i, 1], sh2, sc2, moe_router[j])
            dest, slot_tok, block_e, n_used = _slot_plan(meta, counts)
            xs = _gather_slots(slot_tok, h32)
            t = _moe_up(block_e, n_used, xs, moe_w13, j)
            y = _moe_down(block_e, n_used, t, moe_w2, j)
            xt = _combine(dest, xt, meta, g2, y, final_g, final=(i == DEPTH - 1))
    return xt.reshape(BATCH, SEQ, D_MODEL)
```

```python
import functools

import jax
import jax.numpy as jnp
from jax import lax
from jax.experimental import pallas as pl
from jax.experimental.pallas import tpu as pltpu

D_MODEL = 2048
BATCH = 4
SEQ = 2048
TOKENS = BATCH * SEQ
DEPTH = 4
CHUNK = 64
EPS = 1e-6
CONV_W = 3
GLA_HEADS = 4
GLA_DK = D_MODEL // 2
GLA_DV = D_MODEL
GLA_DK_HEAD = GLA_DK // GLA_HEADS
GLA_DV_HEAD = GLA_DV // GLA_HEADS
GLA_GATE_RANK = 16
GLA_GATE_NORM = 16.0
GLA_MAIN = 2 * GLA_DK + 2 * GLA_DV
D_FF = 5632
N_EXPERTS = 8
TOP_K = 2

LANES = 128
SUBLANES = 8
MXU_DIM = 256
VMEM_LIMIT = 56 * 1024 * 1024

MOE_ROWS = 256
N_SLOT_BLOCKS = TOKENS * TOP_K // MOE_ROWS + N_EXPERTS
N_SLOTS = N_SLOT_BLOCKS * MOE_ROWS

F32 = jnp.float32
BF16 = jnp.bfloat16


def _params(n_axes):
    return pltpu.CompilerParams(
        dimension_semantics=("arbitrary",) * n_axes, vmem_limit_bytes=VMEM_LIMIT)


def _dot(a, b):
    return jnp.dot(a, b, preferred_element_type=F32)


def _silu(v):
    return v * jax.nn.sigmoid(v)


ADA_TN = 1024


def _ada_kernel(c_ref, w_ref, b_ref, o_ref):
    c_act = _silu(c_ref[...]).astype(BF16)
    o_ref[...] = _dot(c_act, w_ref[...].astype(BF16)) + b_ref[...]


def _ada_all(c, ada_w, ada_b):
    c_pad = jnp.pad(c, ((0, SUBLANES - BATCH), (0, 0)))
    n = 6 * D_MODEL
    return pl.pallas_call(
        _ada_kernel,
        grid=(DEPTH, n // ADA_TN),
        in_specs=[
            pl.BlockSpec((SUBLANES, D_MODEL), lambda l, j: (0, 0)),
            pl.BlockSpec((None, D_MODEL, ADA_TN), lambda l, j: (l, 0, j)),
            pl.BlockSpec((None, 1, ADA_TN), lambda l, j: (l, 0, j)),
        ],
        out_specs=pl.BlockSpec((None, SUBLANES, ADA_TN), lambda l, j: (l, 0, j)),
        out_shape=jax.ShapeDtypeStruct((DEPTH, SUBLANES, n), F32),
        compiler_params=_params(2),
        name="ada_mod",
    )(c_pad, ada_w, ada_b.reshape(DEPTH, 1, n))


NORM_TM = 512


def _norm_mod(x, g, sc, sh):
    ms = jnp.mean(x * x, axis=-1, keepdims=True)
    y = x * lax.rsqrt(ms + EPS) * g
    return y * (1.0 + sc) + sh


def _norm_mod_kernel(x_ref, g_ref, sh_ref, sc_ref, h_ref):
    h_ref[...] = _norm_mod(x_ref[...], g_ref[...], sc_ref[...], sh_ref[...]).astype(h_ref.dtype)


def _row_specs(tm):
    per_seq = SEQ // tm
    rows = pl.BlockSpec((tm, D_MODEL), lambda i: (i, 0))
    vec = pl.BlockSpec((1, D_MODEL), lambda i: (0, 0))
    seq_vec = pl.BlockSpec((None, 1, D_MODEL), lambda i: (i // per_seq, 0, 0))
    return rows, vec, seq_vec


def _norm_modulate(x, g, sh, sc):
    rows, vec, seq_vec = _row_specs(NORM_TM)
    return pl.pallas_call(
        _norm_mod_kernel,
        grid=(TOKENS // NORM_TM,),
        in_specs=[rows, vec, seq_vec, seq_vec],
        out_specs=rows,
        out_shape=jax.ShapeDtypeStruct((TOKENS, D_MODEL), BF16),
        compiler_params=_params(1),
        name="norm_mod",
    )(x, g.reshape(1, D_MODEL), sh, sc)


MM_TM = 1024
MM_TN = 512


def _cast_weights_once(w_refs, wb_refs):
    @pl.when(pl.program_id(1) == 0)
    def _():
        for w_ref, wb_ref in zip(w_refs, wb_refs):
            wb_ref[...] = w_ref[...].astype(BF16)


def _mm_kernel(a_ref, w_ref, o_ref, wb_ref):
    _cast_weights_once((w_ref,), (wb_ref,))
    o_ref[...] = _dot(a_ref[...], wb_ref[...]).astype(o_ref.dtype)


def _matmul(a, w3, layer, n_out, out_dtype):
    k = a.shape[1]
    return pl.pallas_call(
        _mm_kernel,
        grid=(n_out // MM_TN, TOKENS // MM_TM),
        in_specs=[
            pl.BlockSpec((MM_TM, k), lambda j, i: (i, 0)),
            pl.BlockSpec((None, k, MM_TN), lambda j, i: (layer, 0, j)),
        ],
        out_specs=pl.BlockSpec((MM_TM, MM_TN), lambda j, i: (i, j)),
        out_shape=jax.ShapeDtypeStruct((TOKENS, n_out), out_dtype),
        scratch_shapes=[pltpu.VMEM((k, MM_TN), BF16)],
        compiler_params=_params(2),
        name="matmul",
    )(a, w3)


RES_ROW_BYTES = 6 * 1024 * 1024


def _mm_res_kernel(a_ref, w_ref, x_ref, g_ref, o_ref, wb_ref):
    _cast_weights_once((w_ref,), (wb_ref,))
    o_ref[...] = x_ref[...] + g_ref[...] * _dot(a_ref[...], wb_ref[...])


def _matmul_residual(a, w3, layer, x, gate):
    k = a.shape[1]
    tm = min(MM_TM, RES_ROW_BYTES // (2 * k) // MXU_DIM * MXU_DIM)
    per_seq = SEQ // tm
    return pl.pallas_call(
        _mm_res_kernel,
        grid=(D_MODEL // MM_TN, TOKENS // tm),
        in_specs=[
            pl.BlockSpec((tm, k), lambda j, i: (i, 0)),
            pl.BlockSpec((None, k, MM_TN), lambda j, i: (layer, 0, j)),
            pl.BlockSpec((tm, MM_TN), lambda j, i: (i, j)),
            pl.BlockSpec((None, 1, MM_TN), lambda j, i: (i // per_seq, 0, j)),
        ],
        out_specs=pl.BlockSpec((tm, MM_TN), lambda j, i: (i, j)),
        out_shape=jax.ShapeDtypeStruct((TOKENS, D_MODEL), F32),
        scratch_shapes=[pltpu.VMEM((k, MM_TN), BF16)],
        compiler_params=_params(2),
        name="matmul_residual",
    )(a, w3, x, gate)


CONV_TM = 1024


def _conv_in_kernel(a_ref, wgb_ref, wgc_ref, wu_ref, kc_ref, o_ref, wb_ref, v_ref):
    i = pl.program_id(1)
    _cast_weights_once((wgb_ref, wgc_ref, wu_ref), (wb_ref.at[0], wb_ref.at[1], wb_ref.at[2]))

    @pl.when(i % (SEQ // CONV_TM) == 0)
    def _():
        v_ref[0:SUBLANES, :] = jnp.zeros((SUBLANES, MM_TN), F32)

    a = a_ref[...]
    gc = _dot(a, wb_ref[1])
    u = _dot(a, wb_ref[2])
    v_ref[SUBLANES:SUBLANES + CONV_TM, :] = gc * u
    conv = kc_ref[2:3, :] * v_ref[SUBLANES:SUBLANES + CONV_TM, :]
    conv = conv + kc_ref[1:2, :] * v_ref[SUBLANES - 1:SUBLANES - 1 + CONV_TM, :]
    conv = conv + kc_ref[0:1, :] * v_ref[SUBLANES - 2:SUBLANES - 2 + CONV_TM, :]
    gb = _dot(a, wb_ref[0])
    o_ref[...] = (gb * conv).astype(o_ref.dtype)
    v_ref[0:SUBLANES, :] = v_ref[CONV_TM:CONV_TM + SUBLANES, :]


def _conv_in(h, conv_w_in, conv_k, layer):
    nb = D_MODEL // MM_TN
    w_spec = lambda off: pl.BlockSpec((None, D_MODEL, MM_TN), lambda j, i: (layer, 0, j + off))
    return pl.pallas_call(
        _conv_in_kernel,
        grid=(nb, TOKENS // CONV_TM),
        in_specs=[
            pl.BlockSpec((CONV_TM, D_MODEL), lambda j, i: (i, 0)),
            w_spec(0), w_spec(nb), w_spec(2 * nb),
            pl.BlockSpec((None, CONV_W, MM_TN), lambda j, i: (layer, 0, j)),
        ],
        out_specs=pl.BlockSpec((CONV_TM, MM_TN), lambda j, i: (i, j)),
        out_shape=jax.ShapeDtypeStruct((TOKENS, D_MODEL), BF16),
        scratch_shapes=[
            pltpu.VMEM((3, D_MODEL, MM_TN), BF16),
            pltpu.VMEM((CONV_TM + SUBLANES, MM_TN), F32),
        ],
        compiler_params=_params(2),
        name="conv_in",
    )(h, conv_w_in, conv_w_in, conv_w_in, conv_k)


def _ffn_up_kernel(a_ref, w1_ref, w3_ref, o_ref, wb_ref):
    _cast_weights_once((w1_ref, w3_ref), (wb_ref.at[0], wb_ref.at[1]))
    a = a_ref[...]
    p = _dot(a, wb_ref[0])
    q = _dot(a, wb_ref[1])
    o_ref[...] = (_silu(p) * q).astype(o_ref.dtype)


def _ffn_up(h, w13, layer):
    nb = D_FF // MM_TN
    w_spec = lambda off: pl.BlockSpec((None, D_MODEL, MM_TN), lambda j, i: (layer, 0, j + off))
    return pl.pallas_call(
        _ffn_up_kernel,
        grid=(nb, TOKENS // MM_TM),
        in_specs=[pl.BlockSpec((MM_TM, D_MODEL), lambda j, i: (i, 0)), w_spec(0), w_spec(nb)],
        out_specs=pl.BlockSpec((MM_TM, MM_TN), lambda j, i: (i, j)),
        out_shape=jax.ShapeDtypeStruct((TOKENS, D_FF), BF16),
        scratch_shapes=[pltpu.VMEM((2, D_MODEL, MM_TN), BF16)],
        compiler_params=_params(2),
        name="ffn_up",
    )(h, w13, w13)


LOGA_TM = 512


def _loga_kernel(h_ref, wl_ref, wgk_ref, bgk_ref, o_ref):
    w_low = wl_ref[:, 0:GLA_GATE_RANK].astype(BF16)
    low = _dot(h_ref[...], w_low)
    z = _dot(low.astype(BF16), wgk_ref[...].astype(BF16)) + bgk_ref[...]
    log_sig = jnp.minimum(z, 0.0) - jnp.log1p(jnp.exp(-jnp.abs(z)))
    o_ref[...] = log_sig * (1.0 / GLA_GATE_NORM)


def _gla_log_decay(h, w_in, layer, w_gk, b_gk):
    return pl.pallas_call(
        _loga_kernel,
        grid=(TOKENS // LOGA_TM,),
        in_specs=[
            pl.BlockSpec((LOGA_TM, D_MODEL), lambda i: (i, 0)),
            pl.BlockSpec((None, D_MODEL, LANES), lambda i: (layer, 0, GLA_MAIN // LANES)),
            pl.BlockSpec((GLA_GATE_RANK, GLA_DK), lambda i: (0, 0)),
            pl.BlockSpec((1, GLA_DK), lambda i: (0, 0)),
        ],
        out_specs=pl.BlockSpec((LOGA_TM, GLA_DK), lambda i: (i, 0)),
        out_shape=jax.ShapeDtypeStruct((TOKENS, GLA_DK), F32),
        compiler_params=_params(1),
        name="gla_log_decay",
    )(h, w_in, w_gk, b_gk.reshape(1, GLA_DK))


GLA_ROWS = 512


def _gla_kernel(q_ref, k_ref, v_ref, g_ref, la_ref, ng_ref, o_ref, st_ref):
    @pl.when(pl.program_id(1) == 0)
    def _():
        st_ref[...] = jnp.zeros(st_ref.shape, F32)

    row = lax.broadcasted_iota(jnp.int32, (CHUNK, CHUNK), 0)
    col = lax.broadcasted_iota(jnp.int32, (CHUNK, CHUNK), 1)
    tri = (col <= row).astype(BF16)

    def chunk(ci, carry):
        rows = pl.ds(pl.multiple_of(ci * CHUNK, CHUNK), CHUNK)
        la = la_ref[rows, :]
        la_hi = la.astype(BF16)
        la_lo = (la - la_hi.astype(F32)).astype(BF16)
        bcum = _dot(tri, la_hi) + _dot(tri, la_lo)
        btot = bcum[CHUNK - 1:CHUNK, :]
        k_dec = (k_ref[rows, :].astype(F32) * jnp.exp(btot - bcum)).astype(BF16)
        decay = jnp.exp(btot)
        q = (q_ref[rows, :].astype(F32) * (GLA_DK_HEAD ** -0.5)).astype(BF16)
        for h in range(GLA_HEADS):
            kcols = slice(h * GLA_DK_HEAD, (h + 1) * GLA_DK_HEAD)
            vcols = slice(h * GLA_DV_HEAD, (h + 1) * GLA_DV_HEAD)
            kv_t = lax.dot_general(v_ref[rows, vcols], k_dec[:, kcols],
                                   (((0,), (0,)), ((), ())), preferred_element_type=F32)
            st = st_ref[h] * decay[:, kcols] + kv_t
            st_ref[h] = st
            o = lax.dot_general(q[:, kcols], st.astype(BF16),
                                (((1,), (1,)), ((), ())), preferred_element_type=F32)
            o = o * lax.rsqrt(jnp.mean(o * o, axis=-1, keepdims=True) + EPS) * ng_ref[...]
            o = o * _silu(g_ref[rows, vcols].astype(F32))
            o_ref[rows, vcols] = o.astype(o_ref.dtype)
        return carry

    lax.fori_loop(0, GLA_ROWS // CHUNK, chunk, 0)


def _gla_scan(proj, log_a, norm_g):
    per_seq = SEQ // GLA_ROWS
    rows = lambda width, blk: pl.BlockSpec((GLA_ROWS, width), lambda b, s: (b * per_seq + s, blk))
    return pl.pallas_call(
        _gla_kernel,
        grid=(BATCH, per_seq),
        in_specs=[
            rows(GLA_DK, 0), rows(GLA_DK, 1), rows(GLA_DV, 1), rows(GLA_DV, 2),
            rows(GLA_DK, 0),
            pl.BlockSpec((1, GLA_DV_HEAD), lambda b, s: (0, 0)),
        ],
        out_specs=rows(GLA_DV, 0),
        out_shape=jax.ShapeDtypeStruct((TOKENS, GLA_DV), BF16),
        scratch_shapes=[pltpu.VMEM((GLA_HEADS, GLA_DV_HEAD, GLA_DK_HEAD), F32)],
        compiler_params=_params(2),
        name="gla_scan",
    )(proj, proj, proj, proj, log_a, norm_g.reshape(1, GLA_DV_HEAD))


ROUTE_TM = 512
META_E0, META_E1, META_G0, META_G1, META_R0, META_R1 = range(6)


def _route_kernel(x_ref, g_ref, sh_ref, sc_ref, r_ref, h_ref, meta_ref, cnt_ref, run_ref):
    @pl.when(pl.program_id(0) == 0)
    def _():
        run_ref[...] = jnp.zeros(run_ref.shape, F32)

    h = _norm_mod(x_ref[...], g_ref[...], sc_ref[...], sh_ref[...])
    bits = pltpu.bitcast(h.astype(BF16).astype(F32), jnp.uint32)
    h_ref[...] = bits[:, D_MODEL // 2:] | (bits[:, :D_MODEL // 2] >> 16)
    logits =jnp.dot(h, r_ref[...], preferred_element_type=F32, precision=lax.Precision.HIGHEST)
    lane = lax.broadcasted_iota(jnp.int32, logits.shape, 1)
    logits = jnp.where(lane < N_EXPERTS, logits, -jnp.inf)
    m0 = jnp.max(logits, axis=1, keepdims=True)
    e0 = jnp.min(jnp.where(logits == m0, lane, LANES), axis=1, keepdims=True)
    rest = jnp.where(lane == e0, -jnp.inf, logits)
    m1 = jnp.max(rest, axis=1, keepdims=True)
    e1 = jnp.min(jnp.where(rest == m1, lane, LANES), axis=1, keepdims=True)
    p = jnp.exp(m1 - m0)
    gate0 = 1.0 / (1.0 + p)
    gate1 = p / (1.0 + p)

    hot0 = (lane == e0).astype(F32)
    hot1 = (lane == e1).astype(F32)
    both = hot0 + hot1
    row = lax.broadcasted_iota(jnp.int32, (ROUTE_TM, ROUTE_TM), 0)
    col = lax.broadcasted_iota(jnp.int32, (ROUTE_TM, ROUTE_TM), 1)
    before = _dot((col < row).astype(BF16), both.astype(BF16)) + run_ref[0:1, :]
    rank0 = jnp.sum(hot0 * before, axis=1, keepdims=True)
    rank1 = jnp.sum(hot1 * before, axis=1, keepdims=True)
    run_ref[...] = run_ref[...] + jnp.sum(both, axis=0, keepdims=True)
    cnt_ref[...] = run_ref[...]

    meta = jnp.zeros(logits.shape, F32)
    for lane_id, val in ((META_E0, e0.astype(F32)), (META_E1, e1.astype(F32)), (META_G0, gate0),
                         (META_G1, gate1), (META_R0, rank0), (META_R1, rank1)):
        meta = jnp.where(lane == lane_id, val, meta)
    meta_ref[...] = meta


def _route(x, g, sh, sc, router):
    rows, vec, seq_vec = _row_specs(ROUTE_TM)
    router = jnp.pad(router, ((0, 0), (0, LANES - N_EXPERTS)))
    return pl.pallas_call(
        _route_kernel,
        grid=(TOKENS // ROUTE_TM,),
        in_specs=[rows, vec, seq_vec, seq_vec, pl.BlockSpec((D_MODEL, LANES), lambda i: (0, 0))],
        out_specs=[
            pl.BlockSpec((ROUTE_TM, D_MODEL // 2), lambda i: (i, 0)),
            pl.BlockSpec((ROUTE_TM, LANES), lambda i: (i, 0)),
            pl.BlockSpec((SUBLANES, LANES), lambda i: (0, 0)),
        ],
        out_shape=[
            jax.ShapeDtypeStruct((TOKENS, D_MODEL // 2), jnp.uint32),
            jax.ShapeDtypeStruct((TOKENS, LANES), F32),
            jax.ShapeDtypeStruct((SUBLANES, LANES), F32),
        ],
        scratch_shapes=[pltpu.VMEM((SUBLANES, LANES), F32)],
        compiler_params=_params(1),
        name="moe_route",
    )(x, g.reshape(1, D_MODEL), sh, sc, router)


def _row_copy(src_hbm, src_row, dst_vmem, dst_row, sem):
    return pltpu.make_async_copy(
        src_hbm.at[pl.ds(src_row, 1), :], dst_vmem.at[pl.ds(dst_row, 1), :], sem)


DMA_UNROLL = 8


def _for_rows(n_rows, fn):
    def body(r0, carry):
        for u in range(DMA_UNROLL):
            fn(r0 * DMA_UNROLL + u)
        return carry

    lax.fori_loop(0, n_rows // DMA_UNROLL, body, 0)


def _gather_kernel(tok_ref, h_hbm, o_ref, buf_ref, sem):
    b = pl.program_id(0)

    def issue(blk):
        slot = blk & 1
        _for_rows(MOE_ROWS, lambda r: _row_copy(
            h_hbm, tok_ref[blk * MOE_ROWS + r], buf_ref.at[slot], r, sem.at[slot]).start())

    @pl.when(b == 0)
    def _():
        issue(b)

    @pl.when(b + 1 < pl.num_programs(0))
    def _():
        issue(b + 1)

    slot = b & 1
    _for_rows(MOE_ROWS, lambda r: _row_copy(h_hbm, 0, buf_ref.at[slot], r, sem.at[slot]).wait())
    words = buf_ref[slot]
    half = D_MODEL // 2
    o_ref[:, :half] = pltpu.bitcast(words << 16, F32).astype(o_ref.dtype)
    o_ref[:, half:] = pltpu.bitcast(words & jnp.uint32(0xFFFF0000), F32).astype(o_ref.dtype)


def _gather_slots(slot_tok, h_packed):
    return pl.pallas_call(
        _gather_kernel,
        grid_spec=pltpu.PrefetchScalarGridSpec(
            num_scalar_prefetch=1,
            grid=(N_SLOT_BLOCKS,),
            in_specs=[pl.BlockSpec(memory_space=pl.ANY)],
            out_specs=pl.BlockSpec((MOE_ROWS, D_MODEL), lambda b, tok: (b, 0)),
            scratch_shapes=[pltpu.VMEM((2, MOE_ROWS, D_MODEL // 2), jnp.uint32),
                            pltpu.SemaphoreType.DMA((2,))],
        ),
        out_shape=jax.ShapeDtypeStruct((N_SLOTS, D_MODEL), BF16),
        compiler_params=_params(1),
        name="moe_gather",
    )(slot_tok, h_packed)


def _grouped_kernel(n_w, compute, first_ref, cnt_ref, a_hbm, *refs):
    w_refs, o_hbm = refs[:n_w], refs[n_w]
    wb_ref, a_buf, o_buf, sem_in, sem_out = refs[n_w + 1:]
    f, e = pl.program_id(0), pl.program_id(1)
    n, first = cnt_ref[e], first_ref[e]
    col = pl.multiple_of(f * MM_TN, MM_TN)

    def copy_in(i, slot):
        return pltpu.make_async_copy(
            a_hbm.at[pl.ds((first + i) * MOE_ROWS, MOE_ROWS), :], a_buf.at[slot], sem_in.at[slot])

    def copy_out(i, slot):
        return pltpu.make_async_copy(
            o_buf.at[slot], o_hbm.at[pl.ds((first + i) * MOE_ROWS, MOE_ROWS), pl.ds(col, MM_TN)],
            sem_out.at[slot])

    for k in range(n_w):
        wb_ref[k] = w_refs[k][...].astype(BF16)

    @pl.when(jnp.logical_and(jnp.logical_and(f == 0, e == 0), n > 0))
    def _():
        copy_in(0, 0).start()

    def block(i, carry):
        slot = i & 1
        copy_in(i, slot).wait()

        @pl.when(i + 1 < n)
        def _():
            copy_in(i + 1, 1 - slot).start()

        @pl.when(i >= 2)
        def _():
            copy_out(i - 2, slot).wait()

        o_buf[slot] = compute(a_buf[slot], wb_ref).astype(o_buf.dtype)
        copy_out(i, slot).start()
        return carry

    lax.fori_loop(0, n, block, 0)

    last_e = e == N_EXPERTS - 1
    e_next = jnp.where(last_e, 0, e + 1)
    is_last_step = jnp.logical_and(last_e, f == pl.num_programs(0) - 1)

    @pl.when(jnp.logical_and(jnp.logical_not(is_last_step), cnt_ref[e_next] > 0))
    def _():
        pltpu.make_async_copy(
            a_hbm.at[pl.ds(first_ref[e_next] * MOE_ROWS, MOE_ROWS), :], a_buf.at[0], sem_in.at[0]).start()

    @pl.when(n >= 2)
    def _():
        copy_out(n - 2, n & 1).wait()

    @pl.when(n >= 1)
    def _():
        copy_out(n - 1, (n - 1) & 1).wait()

    @pl.when(last_e)
    def _():
        o_buf[0] = jnp.zeros(o_buf.shape[1:], o_buf.dtype)

        def zero_block(i, carry):
            copy_out(i, 0).start()
            copy_out(i, 0).wait()
            return carry

        lax.fori_loop(n, N_SLOT_BLOCKS - first, zero_block, 0)


def _grouped_matmul(name, compute, first_blk, n_blk, a, w4, layer, col_offsets, n_out, out_dtype):
    k = a.shape[1]
    tiles = n_out // MM_TN
    w_spec = lambda off: pl.BlockSpec(
        (None, None, k, MM_TN), lambda f, e, first, cnt: (layer, e, 0, f + off // MM_TN))
    n_w = len(col_offsets)
    return pl.pallas_call(
        functools.partial(_grouped_kernel, n_w, compute),
        grid_spec=pltpu.PrefetchScalarGridSpec(
            num_scalar_prefetch=2,
            grid=(tiles, N_EXPERTS),
            in_specs=[pl.BlockSpec(memory_space=pl.ANY)] + [w_spec(off) for off in col_offsets],
            out_specs=pl.BlockSpec(memory_space=pl.ANY),
            scratch_shapes=[
                pltpu.VMEM((n_w, k, MM_TN), BF16),
                pltpu.VMEM((2, MOE_ROWS, k), BF16),
                pltpu.VMEM((2, MOE_ROWS, MM_TN), out_dtype),
                pltpu.SemaphoreType.DMA((2,)),
                pltpu.SemaphoreType.DMA((2,)),
            ],
        ),
        out_shape=jax.ShapeDtypeStruct((N_SLOTS, n_out), out_dtype),
        compiler_params=_params(2),
        name=name,
    )(first_blk, n_blk, a, *([w4] * n_w))


def _swiglu_tile(a, wb_ref):
    return _silu(_dot(a, wb_ref[0])) * _dot(a, wb_ref[1])


def _down_tile(a, wb_ref):
    return _dot(a, wb_ref[0])


COMB_TM = 256


def _combine_kernel(final, dest_ref, x_ref, meta_ref, g2_ref, fg_ref, y_hbm, o_ref, buf_ref, sem):
    i = pl.program_id(0)

    def issue(tile):
        slot = tile & 1
        for k in range(TOP_K):
            _for_rows(COMB_TM, lambda r: _row_copy(
                y_hbm, dest_ref[TOP_K * (tile * COMB_TM + r) + k], buf_ref.at[slot, k], r,
                sem.at[slot]).start())

    @pl.when(i == 0)
    def _():
        issue(i)

    @pl.when(i + 1 < pl.num_programs(0))
    def _():
        issue(i + 1)

    slot = i & 1
    for k in range(TOP_K):
        _for_rows(COMB_TM, lambda r: _row_copy(y_hbm, 0, buf_ref.at[slot, k], r, sem.at[slot]).wait())
    meta = meta_ref[...]
    f = meta[:, META_G0:META_G0 + 1] * buf_ref[slot, 0] + meta[:, META_G1:META_G1 + 1] * buf_ref[slot, 1]
    x_new = x_ref[...] + g2_ref[...] * f
    if final:
        ms = jnp.mean(x_new * x_new, axis=-1, keepdims=True)
        x_new = x_new * lax.rsqrt(ms + EPS) * fg_ref[...]
    o_ref[...] = x_new


def _combine(dest_flat, x, meta, g2, y_slots, final_g, final):
    per_seq = SEQ // COMB_TM
    return pl.pallas_call(
        functools.partial(_combine_kernel, final),
        grid_spec=pltpu.PrefetchScalarGridSpec(
            num_scalar_prefetch=1,
            grid=(TOKENS // COMB_TM,),
            in_specs=[
                pl.BlockSpec((COMB_TM, D_MODEL), lambda i, d: (i, 0)),
                pl.BlockSpec((COMB_TM, LANES), lambda i, d: (i, 0)),
                pl.BlockSpec((None, 1, D_MODEL), lambda i, d: (i // per_seq, 0, 0)),
                pl.BlockSpec((1, D_MODEL), lambda i, d: (0, 0)),
                pl.BlockSpec(memory_space=pl.ANY),
            ],
            out_specs=pl.BlockSpec((COMB_TM, D_MODEL), lambda i, d: (i, 0)),
            scratch_shapes=[pltpu.VMEM((2, TOP_K, COMB_TM, D_MODEL), F32),
                            pltpu.SemaphoreType.DMA((2,))],
        ),
        out_shape=jax.ShapeDtypeStruct((TOKENS, D_MODEL), F32),
        compiler_params=_params(1),
        name="moe_combine",
    )(dest_flat, x, meta, g2, final_g.reshape(1, D_MODEL), y_slots)


def _slot_plan(meta, counts):
    top_e = meta[:, META_E0:META_E1 + 1].astype(jnp.int32)
    rank = meta[:, META_R0:META_R1 + 1].astype(jnp.int32)
    counts = counts[0, :N_EXPERTS].astype(jnp.int32)
    n_blk = (counts + MOE_ROWS - 1) // MOE_ROWS
    first_blk = jnp.cumsum(n_blk) - n_blk
    dest = (first_blk[top_e] * MOE_ROWS + rank).reshape(TOKENS * TOP_K)
    pair_tok = jnp.arange(TOKENS * TOP_K, dtype=jnp.int32) // TOP_K
    slot_tok = jnp.zeros((N_SLOTS,), jnp.int32).at[dest].set(pair_tok)
    return dest, slot_tok, first_blk.astype(jnp.int32), n_blk.astype(jnp.int32)


def kernel(x, c, ada_w, ada_b, norm_g, conv_w_in, conv_k, conv_w_out, gla_w_in, gla_w_gk, gla_b_gk,
           gla_norm_g, gla_w_out, ffn_w13, ffn_w2, moe_router, moe_w13, moe_w2, final_g):
    assert x.shape == (BATCH, SEQ, D_MODEL) and x.dtype == F32
    mod = _ada_all(c, ada_w, ada_b)[:, :BATCH]
    xt = x.reshape(TOKENS, D_MODEL)
    for i in range(DEPTH):
        j = i // 2
        sh1, sc1, g1, sh2, sc2, g2 = (
            mod[i, :, n * D_MODEL:(n + 1) * D_MODEL].reshape(BATCH, 1, D_MODEL) for n in range(6))
        h = _norm_modulate(xt, norm_g[i, 0], sh1, sc1)
        if i % 2 == 0:
            y = _conv_in(h, conv_w_in, conv_k, j)
            xt = _matmul_residual(y, conv_w_out, j, xt, g1)
            h = _norm_modulate(xt, norm_g[i, 1], sh2, sc2)
            t = _ffn_up(h, ffn_w13, j)
            xt = _matmul_residual(t, ffn_w2, j, xt, g2)
        else:
            proj = _matmul(h, gla_w_in, j, GLA_MAIN, BF16)
            log_a = _gla_log_decay(h, gla_w_in, j, gla_w_gk[j], gla_b_gk[j])
            o = _gla_scan(proj, log_a, gla_norm_g[j])
            xt = _matmul_residual(o, gla_w_out, j, xt, g1)
            h_packed, meta, counts = _route(xt, norm_g[i, 1], sh2, sc2, moe_router[j])
            dest, slot_tok, first_blk, n_blk = _slot_plan(meta, counts)
            xs = _gather_slots(slot_tok, h_packed)
            t = _grouped_matmul("moe_up", _swiglu_tile, first_blk, n_blk, xs, moe_w13, j,
                                (0, D_FF), D_FF, BF16)
            y = _grouped_matmul("moe_down", _down_tile, first_blk, n_blk, t, moe_w2, j,
                                (0,), D_MODEL, F32)
            xt = _combine(dest, xt, meta, g2, y, final_g, final=(i == DEPTH - 1))
    return xt.reshape(BATCH, SEQ, D_MODEL)
```

```python
import functools

import jax
import jax.numpy as jnp
from jax import lax
from jax.experimental import pallas as pl
from jax.experimental.pallas import tpu as pltpu

D_MODEL = 2048
BATCH = 4
SEQ = 2048
TOKENS = BATCH * SEQ
DEPTH = 4
CHUNK = 64
EPS = 1e-6
CONV_W = 3
GLA_HEADS = 4
GLA_DK = D_MODEL // 2
GLA_DV = D_MODEL
GLA_DK_HEAD = GLA_DK // GLA_HEADS
GLA_DV_HEAD = GLA_DV // GLA_HEADS
GLA_GATE_RANK = 16
GLA_GATE_NORM = 16.0
GLA_MAIN = 2 * GLA_DK + 2 * GLA_DV
D_FF = 5632
N_EXPERTS = 8
TOP_K = 2

LANES = 128
SUBLANES = 8
MXU_DIM = 256
VMEM_LIMIT = 56 * 1024 * 1024

MOE_ROWS = 256
N_SLOT_BLOCKS = TOKENS * TOP_K // MOE_ROWS + N_EXPERTS
N_SLOTS = N_SLOT_BLOCKS * MOE_ROWS

F32 = jnp.float32
BF16 = jnp.bfloat16


def _params(n_axes):
    return pltpu.CompilerParams(
        dimension_semantics=("arbitrary",) * n_axes, vmem_limit_bytes=VMEM_LIMIT)


def _dot(a, b):
    return jnp.dot(a, b, preferred_element_type=F32)


def _silu(v):
    return v * jax.nn.sigmoid(v)


ADA_TN = 1024


def _ada_kernel(c_ref, w_ref, b_ref, o_ref):
    c_act = _silu(c_ref[...]).astype(BF16)
    o_ref[...] = _dot(c_act, w_ref[...].astype(BF16)) + b_ref[...]


def _ada_all(c, ada_w, ada_b):
    c_pad = jnp.pad(c, ((0, SUBLANES - BATCH), (0, 0)))
    n = 6 * D_MODEL
    return pl.pallas_call(
        _ada_kernel,
        grid=(DEPTH, n // ADA_TN),
        in_specs=[
            pl.BlockSpec((SUBLANES, D_MODEL), lambda l, j: (0, 0)),
            pl.BlockSpec((None, D_MODEL, ADA_TN), lambda l, j: (l, 0, j)),
            pl.BlockSpec((None, 1, ADA_TN), lambda l, j: (l, 0, j)),
        ],
        out_specs=pl.BlockSpec((None, SUBLANES, ADA_TN), lambda l, j: (l, 0, j)),
        out_shape=jax.ShapeDtypeStruct((DEPTH, SUBLANES, n), F32),
        compiler_params=_params(2),
        name="ada_mod",
    )(c_pad, ada_w, ada_b.reshape(DEPTH, 1, n))


NORM_TM = 512


def _norm_mod(x, g, sc, sh):
    ms = jnp.mean(x * x, axis=-1, keepdims=True)
    y = x * lax.rsqrt(ms + EPS) * g
    return y * (1.0 + sc) + sh


def _norm_mod_kernel(x_ref, g_ref, sh_ref, sc_ref, h_ref):
    h_ref[...] = _norm_mod(x_ref[...], g_ref[...], sc_ref[...], sh_ref[...]).astype(h_ref.dtype)


def _row_specs(tm):
    per_seq = SEQ // tm
    rows = pl.BlockSpec((tm, D_MODEL), lambda i: (i, 0))
    vec = pl.BlockSpec((1, D_MODEL), lambda i: (0, 0))
    seq_vec = pl.BlockSpec((None, 1, D_MODEL), lambda i: (i // per_seq, 0, 0))
    return rows, vec, seq_vec


def _norm_modulate(x, g, sh, sc):
    rows, vec, seq_vec = _row_specs(NORM_TM)
    return pl.pallas_call(
        _norm_mod_kernel,
        grid=(TOKENS // NORM_TM,),
        in_specs=[rows, vec, seq_vec, seq_vec],
        out_specs=rows,
        out_shape=jax.ShapeDtypeStruct((TOKENS, D_MODEL), BF16),
        compiler_params=_params(1),
        name="norm_mod",
    )(x, g.reshape(1, D_MODEL), sh, sc)


MM_TM = 1024
MM_TN = 512


def _cast_weights_once(w_refs, wb_refs):
    @pl.when(pl.program_id(1) == 0)
    def _():
        for w_ref, wb_ref in zip(w_refs, wb_refs):
            wb_ref[...] = w_ref[...].astype(BF16)


def _dot_nt(a, b_t):
    return lax.dot_general(a, b_t, (((1,), (1,)), ((), ())), preferred_element_type=F32)


def _mm_nt_kernel(a_ref, wt_ref, o_ref, wb_ref):
    _cast_weights_once((wt_ref,), (wb_ref,))
    o_ref[...] = _dot_nt(a_ref[...], wb_ref[...]).astype(o_ref.dtype)


def _matmul_nt(a, wt3, layer, n_out, out_dtype):
    k = a.shape[1]
    return pl.pallas_call(
        _mm_nt_kernel,
        grid=(n_out // MM_TN, TOKENS // MM_TM),
        in_specs=[
            pl.BlockSpec((MM_TM, k), lambda j, i: (i, 0)),
            pl.BlockSpec((None, MM_TN, k), lambda j, i: (layer, j, 0)),
        ],
        out_specs=pl.BlockSpec((MM_TM, MM_TN), lambda j, i: (i, j)),
        out_shape=jax.ShapeDtypeStruct((TOKENS, n_out), out_dtype),
        scratch_shapes=[pltpu.VMEM((MM_TN, k), BF16)],
        compiler_params=_params(2),
        name="matmul_nt",
    )(a, wt3)


RES_ROW_BYTES = 6 * 1024 * 1024


def _mm_res_kernel(a_ref, w_ref, x_ref, g_ref, o_ref, wb_ref):
    _cast_weights_once((w_ref,), (wb_ref,))
    o_ref[...] = x_ref[...] + g_ref[...] * _dot(a_ref[...], wb_ref[...])


def _matmul_residual(a, w3, layer, x, gate):
    k = a.shape[1]
    tm = min(MM_TM, RES_ROW_BYTES // (2 * k) // MXU_DIM * MXU_DIM)
    per_seq = SEQ // tm
    return pl.pallas_call(
        _mm_res_kernel,
        grid=(D_MODEL // MM_TN, TOKENS // tm),
        in_specs=[
            pl.BlockSpec((tm, k), lambda j, i: (i, 0)),
            pl.BlockSpec((None, k, MM_TN), lambda j, i: (layer, 0, j)),
            pl.BlockSpec((tm, MM_TN), lambda j, i: (i, j)),
            pl.BlockSpec((None, 1, MM_TN), lambda j, i: (i // per_seq, 0, j)),
        ],
        out_specs=pl.BlockSpec((tm, MM_TN), lambda j, i: (i, j)),
        out_shape=jax.ShapeDtypeStruct((TOKENS, D_MODEL), F32),
        scratch_shapes=[pltpu.VMEM((k, MM_TN), BF16)],
        compiler_params=_params(2),
        name="matmul_residual",
    )(a, w3, x, gate)


CONV_TM = 1024


def _conv_in_kernel(a_ref, wgb_ref, wgc_ref, wu_ref, kc_ref, o_ref, wb_ref, v_ref):
    i = pl.program_id(1)
    _cast_weights_once((wgb_ref, wgc_ref, wu_ref), (wb_ref.at[0], wb_ref.at[1], wb_ref.at[2]))

    @pl.when(i % (SEQ // CONV_TM) == 0)
    def _():
        v_ref[0:SUBLANES, :] = jnp.zeros((SUBLANES, MM_TN), F32)

    a = a_ref[...]
    gc = _dot(a, wb_ref[1])
    u = _dot(a, wb_ref[2])
    v_ref[SUBLANES:SUBLANES + CONV_TM, :] = gc * u
    conv = kc_ref[2:3, :] * v_ref[SUBLANES:SUBLANES + CONV_TM, :]
    conv = conv + kc_ref[1:2, :] * v_ref[SUBLANES - 1:SUBLANES - 1 + CONV_TM, :]
    conv = conv + kc_ref[0:1, :] * v_ref[SUBLANES - 2:SUBLANES - 2 + CONV_TM, :]
    gb = _dot(a, wb_ref[0])
    o_ref[...] = (gb * conv).astype(o_ref.dtype)
    v_ref[0:SUBLANES, :] = v_ref[CONV_TM:CONV_TM + SUBLANES, :]


def _conv_in(h, conv_w_in, conv_k, layer):
    nb = D_MODEL // MM_TN
    w_spec = lambda off: pl.BlockSpec((None, D_MODEL, MM_TN), lambda j, i: (layer, 0, j + off))
    return pl.pallas_call(
        _conv_in_kernel,
        grid=(nb, TOKENS // CONV_TM),
        in_specs=[
            pl.BlockSpec((CONV_TM, D_MODEL), lambda j, i: (i, 0)),
            w_spec(0), w_spec(nb), w_spec(2 * nb),
            pl.BlockSpec((None, CONV_W, MM_TN), lambda j, i: (layer, 0, j)),
        ],
        out_specs=pl.BlockSpec((CONV_TM, MM_TN), lambda j, i: (i, j)),
        out_shape=jax.ShapeDtypeStruct((TOKENS, D_MODEL), BF16),
        scratch_shapes=[
            pltpu.VMEM((3, D_MODEL, MM_TN), BF16),
            pltpu.VMEM((CONV_TM + SUBLANES, MM_TN), F32),
        ],
        compiler_params=_params(2),
        name="conv_in",
    )(h, conv_w_in, conv_w_in, conv_w_in, conv_k)


def _ffn_up_kernel(a_ref, w1_ref, w3_ref, o_ref, wb_ref):
    _cast_weights_once((w1_ref, w3_ref), (wb_ref.at[0], wb_ref.at[1]))
    a = a_ref[...]
    p = _dot(a, wb_ref[0])
    q = _dot(a, wb_ref[1])
    o_ref[...] = (_silu(p) * q).astype(o_ref.dtype)


def _ffn_up(h, w13, layer):
    nb = D_FF // MM_TN
    w_spec = lambda off: pl.BlockSpec((None, D_MODEL, MM_TN), lambda j, i: (layer, 0, j + off))
    return pl.pallas_call(
        _ffn_up_kernel,
        grid=(nb, TOKENS // MM_TM),
        in_specs=[pl.BlockSpec((MM_TM, D_MODEL), lambda j, i: (i, 0)), w_spec(0), w_spec(nb)],
        out_specs=pl.BlockSpec((MM_TM, MM_TN), lambda j, i: (i, j)),
        out_shape=jax.ShapeDtypeStruct((TOKENS, D_FF), BF16),
        scratch_shapes=[pltpu.VMEM((2, D_MODEL, MM_TN), BF16)],
        compiler_params=_params(2),
        name="ffn_up",
    )(h, w13, w13)


LOGA_TM = 512


def _loga_kernel(h_ref, wl_ref, wgk_ref, bgk_ref, o_ref):
    low = _dot_nt(h_ref[...], wl_ref[...].astype(BF16))
    z =_dot(low.astype(BF16), wgk_ref[...].astype(BF16)) + bgk_ref[...]
    log_sig = jnp.minimum(z, 0.0) - jnp.log1p(jnp.exp(-jnp.abs(z)))
    o_ref[...] = log_sig * (1.0 / GLA_GATE_NORM)


def _gla_log_decay(h, w_in_t, layer, w_gk, b_gk):
    return pl.pallas_call(
        _loga_kernel,
        grid=(TOKENS // LOGA_TM,),
        in_specs=[
            pl.BlockSpec((LOGA_TM, D_MODEL), lambda i: (i, 0)),
            pl.BlockSpec((None, GLA_GATE_RANK, D_MODEL), lambda i: (layer, GLA_MAIN // GLA_GATE_RANK, 0)),
            pl.BlockSpec((GLA_GATE_RANK, GLA_DK), lambda i: (0, 0)),
            pl.BlockSpec((1, GLA_DK), lambda i: (0, 0)),
        ],
        out_specs=pl.BlockSpec((LOGA_TM, GLA_DK), lambda i: (i, 0)),
        out_shape=jax.ShapeDtypeStruct((TOKENS, GLA_DK), F32),
        compiler_params=_params(1),
        name="gla_log_decay",
    )(h, w_in_t, w_gk, b_gk.reshape(1, GLA_DK))


GLA_ROWS = 512


def _gla_kernel(q_ref, k_ref, v_ref, g_ref, la_ref, ng_ref, o_ref, st_ref):
    @pl.when(pl.program_id(1) == 0)
    def _():
        st_ref[...] = jnp.zeros(st_ref.shape, F32)

    row = lax.broadcasted_iota(jnp.int32, (CHUNK, CHUNK), 0)
    col = lax.broadcasted_iota(jnp.int32, (CHUNK, CHUNK), 1)
    tri = (col <= row).astype(BF16)

    def chunk(ci, carry):
        rows = pl.ds(pl.multiple_of(ci * CHUNK, CHUNK), CHUNK)
        la = la_ref[rows, :]
        la_hi = la.astype(BF16)
        la_lo = (la - la_hi.astype(F32)).astype(BF16)
        bcum = _dot(tri, la_hi) + _dot(tri, la_lo)
        btot = bcum[CHUNK - 1:CHUNK, :]
        k_dec = (k_ref[rows, :].astype(F32) * jnp.exp(btot - bcum)).astype(BF16)
        decay = jnp.exp(btot)
        q = (q_ref[rows, :].astype(F32) * (GLA_DK_HEAD ** -0.5)).astype(BF16)
        for h in range(GLA_HEADS):
            kcols = slice(h * GLA_DK_HEAD, (h + 1) * GLA_DK_HEAD)
            vcols = slice(h * GLA_DV_HEAD, (h + 1) * GLA_DV_HEAD)
            kv_t = lax.dot_general(v_ref[rows, vcols], k_dec[:, kcols],
                                   (((0,), (0,)), ((), ())), preferred_element_type=F32)
            st = st_ref[h] * decay[:, kcols] + kv_t
            st_ref[h] = st
            o = lax.dot_general(q[:, kcols], st.astype(BF16),
                                (((1,), (1,)), ((), ())), preferred_element_type=F32)
            o = o * lax.rsqrt(jnp.mean(o * o, axis=-1, keepdims=True) + EPS) * ng_ref[...]
            o = o * _silu(g_ref[rows, vcols].astype(F32))
            o_ref[rows, vcols] = o.astype(o_ref.dtype)
        return carry

    lax.fori_loop(0, GLA_ROWS // CHUNK, chunk, 0)


def _gla_scan(proj, log_a, norm_g):
    per_seq = SEQ // GLA_ROWS
    rows = lambda width, blk: pl.BlockSpec((GLA_ROWS, width), lambda b, s: (b * per_seq + s, blk))
    return pl.pallas_call(
        _gla_kernel,
        grid=(BATCH, per_seq),
        in_specs=[
            rows(GLA_DK, 0), rows(GLA_DK, 1), rows(GLA_DV, 1), rows(GLA_DV, 2),
            rows(GLA_DK, 0),
            pl.BlockSpec((1, GLA_DV_HEAD), lambda b, s: (0, 0)),
        ],
        out_specs=rows(GLA_DV, 0),
        out_shape=jax.ShapeDtypeStruct((TOKENS, GLA_DV), BF16),
        scratch_shapes=[pltpu.VMEM((GLA_HEADS, GLA_DV_HEAD, GLA_DK_HEAD), F32)],
        compiler_params=_params(2),
        name="gla_scan",
    )(proj, proj, proj, proj, log_a, norm_g.reshape(1, GLA_DV_HEAD))


ROUTE_TM = 512
META_E0, META_E1, META_G0, META_G1, META_R0, META_R1 = range(6)


def _route_kernel(x_ref, g_ref, sh_ref, sc_ref, r_ref, h_ref, meta_ref, cnt_ref, run_ref):
    @pl.when(pl.program_id(0) == 0)
    def _():
        run_ref[...] = jnp.zeros(run_ref.shape, F32)

    h = _norm_mod(x_ref[...], g_ref[...], sc_ref[...], sh_ref[...])
    h_ref[...] = h
    logits =jnp.dot(h, r_ref[...], preferred_element_type=F32, precision=lax.Precision.HIGHEST)
    lane = lax.broadcasted_iota(jnp.int32, logits.shape, 1)
    logits = jnp.where(lane < N_EXPERTS, logits, -jnp.inf)
    m0 = jnp.max(logits, axis=1, keepdims=True)
    e0 = jnp.min(jnp.where(logits == m0, lane, LANES), axis=1, keepdims=True)
    rest = jnp.where(lane == e0, -jnp.inf, logits)
    m1 = jnp.max(rest, axis=1, keepdims=True)
    e1 = jnp.min(jnp.where(rest == m1, lane, LANES), axis=1, keepdims=True)
    p = jnp.exp(m1 - m0)
    gate0 = 1.0 / (1.0 + p)
    gate1 = p / (1.0 + p)

    hot0 = (lane == e0).astype(F32)
    hot1 = (lane == e1).astype(F32)
    both = hot0 + hot1
    row = lax.broadcasted_iota(jnp.int32, (ROUTE_TM, ROUTE_TM), 0)
    col = lax.broadcasted_iota(jnp.int32, (ROUTE_TM, ROUTE_TM), 1)
    before = _dot((col < row).astype(BF16), both.astype(BF16)) + run_ref[0:1, :]
    rank0 = jnp.sum(hot0 * before, axis=1, keepdims=True)
    rank1 = jnp.sum(hot1 * before, axis=1, keepdims=True)
    run_ref[...] = run_ref[...] + jnp.sum(both, axis=0, keepdims=True)
    cnt_ref[...] = run_ref[...]

    meta = jnp.zeros(logits.shape, F32)
    for lane_id, val in ((META_E0, e0.astype(F32)), (META_E1, e1.astype(F32)), (META_G0, gate0),
                         (META_G1, gate1), (META_R0, rank0), (META_R1, rank1)):
        meta = jnp.where(lane == lane_id, val, meta)
    meta_ref[...] = meta


def _route(x, g, sh, sc, router):
    rows, vec, seq_vec = _row_specs(ROUTE_TM)
    router = jnp.pad(router, ((0, 0), (0, LANES - N_EXPERTS)))
    return pl.pallas_call(
        _route_kernel,
        grid=(TOKENS // ROUTE_TM,),
        in_specs=[rows, vec, seq_vec, seq_vec, pl.BlockSpec((D_MODEL, LANES), lambda i: (0, 0))],
        out_specs=[
            rows,
            pl.BlockSpec((ROUTE_TM, LANES), lambda i: (i, 0)),
            pl.BlockSpec((SUBLANES, LANES), lambda i: (0, 0)),
        ],
        out_shape=[
            jax.ShapeDtypeStruct((TOKENS, D_MODEL), F32),
            jax.ShapeDtypeStruct((TOKENS, LANES), F32),
            jax.ShapeDtypeStruct((SUBLANES, LANES), F32),
        ],
        scratch_shapes=[pltpu.VMEM((SUBLANES, LANES), F32)],
        compiler_params=_params(1),
        name="moe_route",
    )(x, g.reshape(1, D_MODEL), sh, sc, router)


def _row_copy(src_hbm, src_row, dst_vmem, dst_row, sem):
    return pltpu.make_async_copy(
        src_hbm.at[pl.ds(src_row, 1), :], dst_vmem.at[pl.ds(dst_row, 1), :], sem)


DMA_UNROLL = 8


def _for_rows(n_rows, fn):
    def body(r0, carry):
        for u in range(DMA_UNROLL):
            fn(r0 * DMA_UNROLL + u)
        return carry

    lax.fori_loop(0, n_rows // DMA_UNROLL, body, 0)


def _gather_kernel(tok_ref, h_hbm, o_ref, buf_ref, sem):
    b = pl.program_id(0)

    def issue(blk):
        slot = blk & 1
        _for_rows(MOE_ROWS, lambda r: _row_copy(
            h_hbm, tok_ref[blk * MOE_ROWS + r], buf_ref.at[slot], r, sem.at[slot]).start())

    @pl.when(b == 0)
    def _():
        issue(b)

    @pl.when(b + 1 < pl.num_programs(0))
    def _():
        issue(b + 1)

    slot = b & 1
    _for_rows(MOE_ROWS, lambda r: _row_copy(h_hbm, 0, buf_ref.at[slot], r, sem.at[slot]).wait())
    o_ref[...] = buf_ref[slot].astype(o_ref.dtype)


def _gather_slots(slot_tok, h):
    return pl.pallas_call(
        _gather_kernel,
        grid_spec=pltpu.PrefetchScalarGridSpec(
            num_scalar_prefetch=1,
            grid=(N_SLOT_BLOCKS,),
            in_specs=[pl.BlockSpec(memory_space=pl.ANY)],
            out_specs=pl.BlockSpec((MOE_ROWS, D_MODEL), lambda b, tok: (b, 0)),
            scratch_shapes=[pltpu.VMEM((2, MOE_ROWS, D_MODEL), F32), pltpu.SemaphoreType.DMA((2,))],
        ),
        out_shape=jax.ShapeDtypeStruct((N_SLOTS, D_MODEL), BF16),
        compiler_params=_params(1),
        name="moe_gather",
    )(slot_tok, h)


def _slot_block(ref, blk, slots_on_rows, col=None):
    slots = pl.ds(blk * MOE_ROWS, MOE_ROWS)
    other = slice(None) if col is None else pl.ds(col, MM_TN)
    return ref.at[slots, other] if slots_on_rows else ref.at[other, slots]


def _grouped_kernel(n_w, compute, in_rows, out_rows, first_ref, cnt_ref, a_hbm, *refs):
    w_refs, o_hbm = refs[:n_w], refs[n_w]
    wt_ref, a_buf, o_buf, sem_in, sem_out = refs[n_w + 1:]
    f, e = pl.program_id(0), pl.program_id(1)
    n, first = cnt_ref[e], first_ref[e]
    col = pl.multiple_of(f * MM_TN, MM_TN)

    def copy_in(blk, slot):
        return pltpu.make_async_copy(_slot_block(a_hbm, blk, in_rows), a_buf.at[slot], sem_in.at[slot])

    def copy_out(blk, slot):
        return pltpu.make_async_copy(
            o_buf.at[slot], _slot_block(o_hbm, blk, out_rows, col), sem_out.at[slot])

    for k in range(n_w):
        wt_ref[k * MM_TN:(k + 1) * MM_TN, :] = w_refs[k][...].T.astype(BF16)

    @pl.when(jnp.logical_and(jnp.logical_and(f == 0, e == 0), n > 0))
    def _():
        copy_in(first, 0).start()

    def block(i, carry):
        slot = i & 1
        copy_in(first + i, slot).wait()

        @pl.when(i + 1 < n)
        def _():
            copy_in(first + i + 1, 1 - slot).start()

        @pl.when(i >= 2)
        def _():
            copy_out(first + i - 2, slot).wait()

        o_buf[slot] = compute(wt_ref[...], a_buf[slot]).astype(o_buf.dtype)
        copy_out(first + i, slot).start()
        return carry

    lax.fori_loop(0, n, block, 0)

    last_e = e == N_EXPERTS - 1
    e_next = jnp.where(last_e, 0, e + 1)
    is_last_step = jnp.logical_and(last_e, f == pl.num_programs(0) - 1)

    @pl.when(jnp.logical_and(jnp.logical_not(is_last_step), cnt_ref[e_next] > 0))
    def _():
        copy_in(first_ref[e_next], 0).start()

    @pl.when(n >= 2)
    def _():
        copy_out(first + n - 2, n & 1).wait()

    @pl.when(n >= 1)
    def _():
        copy_out(first + n - 1, (n - 1) & 1).wait()

    @pl.when(last_e)
    def _():
        o_buf[0] = jnp.zeros(o_buf.shape[1:], o_buf.dtype)

        def zero_block(blk, carry):
            copy_out(blk, 0).start()
            copy_out(blk, 0).wait()
            return carry

        lax.fori_loop(first + n, N_SLOT_BLOCKS, zero_block, 0)


def _grouped_matmul(name, compute, first_blk, n_blk, a, in_rows, w4, layer, col_offsets, n_out,
                    out_rows, out_dtype):
    k = a.shape[1] if in_rows else a.shape[0]
    w_spec = lambda off: pl.BlockSpec(
        (None, None, k, MM_TN), lambda f, e, first, cnt: (layer, e, 0, f + off // MM_TN))
    n_w = len(col_offsets)
    a_block = (MOE_ROWS, k) if in_rows else (k, MOE_ROWS)
    o_block = (MOE_ROWS, MM_TN) if out_rows else (MM_TN, MOE_ROWS)
    out_shape = (N_SLOTS, n_out) if out_rows else (n_out, N_SLOTS)
    return pl.pallas_call(
        functools.partial(_grouped_kernel, n_w, compute, in_rows, out_rows),
        grid_spec=pltpu.PrefetchScalarGridSpec(
            num_scalar_prefetch=2,
            grid=(n_out // MM_TN, N_EXPERTS),
            in_specs=[pl.BlockSpec(memory_space=pl.ANY)] + [w_spec(off) for off in col_offsets],
            out_specs=pl.BlockSpec(memory_space=pl.ANY),
            scratch_shapes=[
                pltpu.VMEM((n_w * MM_TN, k), BF16),
                pltpu.VMEM((2,) + a_block, BF16),
                pltpu.VMEM((2,) + o_block, out_dtype),
                pltpu.SemaphoreType.DMA((2,)),
                pltpu.SemaphoreType.DMA((2,)),
            ],
        ),
        out_shape=jax.ShapeDtypeStruct(out_shape, out_dtype),
        compiler_params=_params(2),
        name=name,
    )(first_blk, n_blk, a, *([w4] * n_w))


def _swiglu_tile_t(wt, a):
    pq = lax.dot_general(wt, a, (((1,), (1,)), ((), ())), preferred_element_type=F32)
    return _silu(pq[:MM_TN]) * pq[MM_TN:]


def _down_tile(wt, a_t):
    return _dot(wt, a_t).T


COMB_TM = 256


def _combine_kernel(final, dest_ref, x_ref, meta_ref, g2_ref, fg_ref, y_hbm, o_ref, buf_ref, sem):
    i = pl.program_id(0)

    def issue(tile):
        slot = tile & 1
        for k in range(TOP_K):
            _for_rows(COMB_TM, lambda r: _row_copy(
                y_hbm, dest_ref[TOP_K * (tile * COMB_TM + r) + k], buf_ref.at[slot, k], r,
                sem.at[slot]).start())

    @pl.when(i == 0)
    def _():
        issue(i)

    @pl.when(i + 1 < pl.num_programs(0))
    def _():
        issue(i + 1)

    slot = i & 1
    for k in range(TOP_K):
        _for_rows(COMB_TM, lambda r: _row_copy(y_hbm, 0, buf_ref.at[slot, k], r, sem.at[slot]).wait())
    meta = meta_ref[...]
    f = meta[:, META_G0:META_G0 + 1] * buf_ref[slot, 0] + meta[:, META_G1:META_G1 + 1] * buf_ref[slot, 1]
    x_new = x_ref[...] + g2_ref[...] * f
    if final:
        ms = jnp.mean(x_new * x_new, axis=-1, keepdims=True)
        x_new = x_new * lax.rsqrt(ms + EPS) * fg_ref[...]
    o_ref[...] = x_new


def _combine(dest_flat, x, meta, g2, y_slots, final_g, final):
    per_seq = SEQ // COMB_TM
    return pl.pallas_call(
        functools.partial(_combine_kernel, final),
        grid_spec=pltpu.PrefetchScalarGridSpec(
            num_scalar_prefetch=1,
            grid=(TOKENS // COMB_TM,),
            in_specs=[
                pl.BlockSpec((COMB_TM, D_MODEL), lambda i, d: (i, 0)),
                pl.BlockSpec((COMB_TM, LANES), lambda i, d: (i, 0)),
                pl.BlockSpec((None, 1, D_MODEL), lambda i, d: (i // per_seq, 0, 0)),
                pl.BlockSpec((1, D_MODEL), lambda i, d: (0, 0)),
                pl.BlockSpec(memory_space=pl.ANY),
            ],
            out_specs=pl.BlockSpec((COMB_TM, D_MODEL), lambda i, d: (i, 0)),
            scratch_shapes=[pltpu.VMEM((2, TOP_K, COMB_TM, D_MODEL), F32),
                            pltpu.SemaphoreType.DMA((2,))],
        ),
        out_shape=jax.ShapeDtypeStruct((TOKENS, D_MODEL), F32),
        compiler_params=_params(1),
        name="moe_combine",
    )(dest_flat, x, meta, g2, final_g.reshape(1, D_MODEL), y_slots)


def _slot_plan(meta, counts):
    top_e = meta[:, META_E0:META_E1 + 1].astype(jnp.int32)
    rank = meta[:, META_R0:META_R1 + 1].astype(jnp.int32)
    counts = counts[0, :N_EXPERTS].astype(jnp.int32)
    n_blk = (counts + MOE_ROWS - 1) // MOE_ROWS
    first_blk = jnp.cumsum(n_blk) - n_blk
    dest = (first_blk[top_e] * MOE_ROWS + rank).reshape(TOKENS * TOP_K)
    pair_tok = jnp.arange(TOKENS * TOP_K, dtype=jnp.int32) // TOP_K
    slot_tok = jnp.zeros((N_SLOTS,), jnp.int32).at[dest].set(pair_tok)
    return dest, slot_tok, first_blk.astype(jnp.int32), n_blk.astype(jnp.int32)


def kernel(x, c, ada_w, ada_b, norm_g, conv_w_in, conv_k, conv_w_out, gla_w_in, gla_w_gk, gla_b_gk,
           gla_norm_g, gla_w_out, ffn_w13, ffn_w2, moe_router, moe_w13, moe_w2, final_g):
    assert x.shape == (BATCH, SEQ, D_MODEL) and x.dtype == F32
    mod = _ada_all(c, ada_w, ada_b)[:, :BATCH]
    xt = x.reshape(TOKENS, D_MODEL)
    gla_w_in_t = jnp.swapaxes(gla_w_in, 1, 2)
    for i in range(DEPTH):
        j = i // 2
        sh1, sc1, g1, sh2, sc2, g2 = (
            mod[i, :, n * D_MODEL:(n + 1) * D_MODEL].reshape(BATCH, 1, D_MODEL) for n in range(6))
        h = _norm_modulate(xt, norm_g[i, 0], sh1, sc1)
        if i % 2 == 0:
            y = _conv_in(h, conv_w_in, conv_k, j)
            xt = _matmul_residual(y, conv_w_out, j, xt, g1)
            h = _norm_modulate(xt, norm_g[i, 1], sh2, sc2)
            t = _ffn_up(h, ffn_w13, j)
            xt = _matmul_residual(t, ffn_w2, j, xt, g2)
        else:
            proj = _matmul_nt(h, gla_w_in_t, j, GLA_MAIN, BF16)
            log_a = _gla_log_decay(h, gla_w_in_t, j, gla_w_gk[j], gla_b_gk[j])
            o = _gla_scan(proj, log_a, gla_norm_g[j])
            xt = _matmul_residual(o, gla_w_out, j, xt, g1)
            h32, meta, counts = _route(xt, norm_g[i, 1], sh2, sc2, moe_router[j])
            dest, slot_tok, first_blk, n_blk = _slot_plan(meta, counts)
            xs = _gather_slots(slot_tok, h32)
            t_t = _grouped_matmul("moe_up", _swiglu_tile_t, first_blk, n_blk, xs, True, moe_w13, j,
                                  (0, D_FF), D_FF, False, BF16)
            y = _grouped_matmul("moe_down", _down_tile, first_blk, n_blk, t_t, False, moe_w2, j,
                                (0,), D_MODEL, True, F32)
            xt = _combine(dest, xt, meta, g2, y, final_g, final=(i == DEPTH - 1))
    return xt.reshape(BATCH, SEQ, D_MODEL)
```

```python
import functools

import jax
import jax.numpy as jnp
from jax import lax
from jax.experimental import pallas as pl
from jax.experimental.pallas import tpu as pltpu

D_MODEL = 2048
BATCH = 4
SEQ = 2048
TOKENS = BATCH * SEQ
DEPTH = 4
CHUNK = 64
EPS = 1e-6
CONV_W = 3
GLA_HEADS = 4
GLA_DK = D_MODEL // 2
GLA_DV = D_MODEL
GLA_DK_HEAD = GLA_DK // GLA_HEADS
GLA_DV_HEAD = GLA_DV // GLA_HEADS
GLA_GATE_RANK = 16
GLA_GATE_NORM = 16.0
GLA_MAIN = 2 * GLA_DK + 2 * GLA_DV
D_FF = 5632
N_EXPERTS = 8
TOP_K = 2

LANES = 128
SUBLANES = 8
MXU_DIM = 256
VMEM_LIMIT = 56 * 1024 * 1024

MOE_ROWS = 256
N_SLOT_BLOCKS = TOKENS * TOP_K // MOE_ROWS + N_EXPERTS
N_SLOTS = N_SLOT_BLOCKS * MOE_ROWS

F32 = jnp.float32
BF16 = jnp.bfloat16


def _params(n_axes):
    return pltpu.CompilerParams(
        dimension_semantics=("arbitrary",) * n_axes, vmem_limit_bytes=VMEM_LIMIT)


def _dot(a, b):
    return jnp.dot(a, b, preferred_element_type=F32)


def _silu(v):
    return v * jax.nn.sigmoid(v)


ADA_TN = 1024


def _ada_kernel(c_ref, w_ref, b_ref, o_ref):
    c_act = _silu(c_ref[...]).astype(BF16)
    o_ref[...] = _dot(c_act, w_ref[...].astype(BF16)) + b_ref[...]


def _ada_all(c, ada_w, ada_b):
    c_pad = jnp.pad(c, ((0, SUBLANES - BATCH), (0, 0)))
    n = 6 * D_MODEL
    return pl.pallas_call(
        _ada_kernel,
        grid=(DEPTH, n // ADA_TN),
        in_specs=[
            pl.BlockSpec((SUBLANES, D_MODEL), lambda l, j: (0, 0)),
            pl.BlockSpec((None, D_MODEL, ADA_TN), lambda l, j: (l, 0, j)),
            pl.BlockSpec((None, 1, ADA_TN), lambda l, j: (l, 0, j)),
        ],
        out_specs=pl.BlockSpec((None, SUBLANES, ADA_TN), lambda l, j: (l, 0, j)),
        out_shape=jax.ShapeDtypeStruct((DEPTH, SUBLANES, n), F32),
        compiler_params=_params(2),
        name="ada_mod",
    )(c_pad, ada_w, ada_b.reshape(DEPTH, 1, n))


NORM_TM = 512


def _norm_mod(x, g, sc, sh):
    ms = jnp.mean(x * x, axis=-1, keepdims=True)
    y = x * lax.rsqrt(ms + EPS) * g
    return y * (1.0 + sc) + sh


def _norm_mod_kernel(x_ref, g_ref, sh_ref, sc_ref, h_ref):
    h_ref[...] = _norm_mod(x_ref[...], g_ref[...], sc_ref[...], sh_ref[...]).astype(h_ref.dtype)


def _row_specs(tm):
    per_seq = SEQ // tm
    rows = pl.BlockSpec((tm, D_MODEL), lambda i: (i, 0))
    vec = pl.BlockSpec((1, D_MODEL), lambda i: (0, 0))
    seq_vec = pl.BlockSpec((None, 1, D_MODEL), lambda i: (i // per_seq, 0, 0))
    return rows, vec, seq_vec


def _norm_modulate(x, g, sh, sc):
    rows, vec, seq_vec = _row_specs(NORM_TM)
    return pl.pallas_call(
        _norm_mod_kernel,
        grid=(TOKENS // NORM_TM,),
        in_specs=[rows, vec, seq_vec, seq_vec],
        out_specs=rows,
        out_shape=jax.ShapeDtypeStruct((TOKENS, D_MODEL), BF16),
        compiler_params=_params(1),
        name="norm_mod",
    )(x, g.reshape(1, D_MODEL), sh, sc)


MM_TM = 1024
MM_TN = 512


def _cast_weights_once(w_refs, wb_refs):
    @pl.when(pl.program_id(1) == 0)
    def _():
        for w_ref, wb_ref in zip(w_refs, wb_refs):
            wb_ref[...] = w_ref[...].astype(BF16)


def _dot_nt(a, b_t):
    return lax.dot_general(a, b_t, (((1,), (1,)), ((), ())), preferred_element_type=F32)


def _mm_nt_kernel(a_ref, wt_ref, o_ref, wb_ref):
    _cast_weights_once((wt_ref,), (wb_ref,))
    o_ref[...] = _dot_nt(a_ref[...], wb_ref[...]).astype(o_ref.dtype)


def _matmul_nt(a, wt3, layer, n_out, out_dtype):
    k = a.shape[1]
    tn = _wide_col_tile(k)
    return pl.pallas_call(
        _mm_nt_kernel,
        grid=(n_out // tn, TOKENS // MM_TM),
        in_specs=[
            pl.BlockSpec((MM_TM, k), lambda j, i: (i, 0)),
            pl.BlockSpec((None, tn, k), lambda j, i: (layer, j, 0)),
        ],
        out_specs=pl.BlockSpec((MM_TM, tn), lambda j, i: (i, j)),
        out_shape=jax.ShapeDtypeStruct((TOKENS, n_out), out_dtype),
        scratch_shapes=[pltpu.VMEM((tn, k), BF16)],
        compiler_params=_params(2),
        name="matmul_nt",
    )(a, wt3)


RES_ROW_BYTES = 6 * 1024 * 1024
WIDE_TILE_BYTES = 12 * 1024 * 1024


def _wide_col_tile(k):
    return min(2 * MM_TN, WIDE_TILE_BYTES // (4 * k) // MXU_DIM * MXU_DIM)


def _mm_res_kernel(a_ref, w_ref, x_ref, g_ref, o_ref, wb_ref):
    _cast_weights_once((w_ref,), (wb_ref,))
    o_ref[...] = x_ref[...] + g_ref[...] * _dot(a_ref[...], wb_ref[...])


def _matmul_residual(a, w3, layer, x, gate):
    k = a.shape[1]
    tm = min(MM_TM, RES_ROW_BYTES // (2 * k) // MXU_DIM * MXU_DIM)
    tn = _wide_col_tile(k)
    per_seq = SEQ // tm
    return pl.pallas_call(
        _mm_res_kernel,
        grid=(D_MODEL // tn, TOKENS // tm),
        in_specs=[
            pl.BlockSpec((tm, k), lambda j, i: (i, 0)),
            pl.BlockSpec((None, k, tn), lambda j, i: (layer, 0, j)),
            pl.BlockSpec((tm, tn), lambda j, i: (i, j)),
            pl.BlockSpec((None, 1, tn), lambda j, i: (i // per_seq, 0, j)),
        ],
        out_specs=pl.BlockSpec((tm, tn), lambda j, i: (i, j)),
        out_shape=jax.ShapeDtypeStruct((TOKENS, D_MODEL), F32),
        scratch_shapes=[pltpu.VMEM((k, tn), BF16)],
        compiler_params=_params(2),
        name="matmul_residual",
    )(a, w3, x, gate)


CONV_TM = 1024


def _conv_in_kernel(a_ref, wgb_ref, wgc_ref, wu_ref, kc_ref, o_ref, wb_ref, v_ref):
    i = pl.program_id(1)
    _cast_weights_once((wgb_ref, wgc_ref, wu_ref), (wb_ref.at[0], wb_ref.at[1], wb_ref.at[2]))

    @pl.when(i % (SEQ // CONV_TM) == 0)
    def _():
        v_ref[0:SUBLANES, :] = jnp.zeros((SUBLANES, MM_TN), F32)

    a = a_ref[...]
    gc = _dot(a, wb_ref[1])
    u = _dot(a, wb_ref[2])
    v_ref[SUBLANES:SUBLANES + CONV_TM, :] = gc * u
    conv = kc_ref[2:3, :] * v_ref[SUBLANES:SUBLANES + CONV_TM, :]
    conv = conv + kc_ref[1:2, :] * v_ref[SUBLANES - 1:SUBLANES - 1 + CONV_TM, :]
    conv = conv + kc_ref[0:1, :] * v_ref[SUBLANES - 2:SUBLANES - 2 + CONV_TM, :]
    gb = _dot(a, wb_ref[0])
    o_ref[...] = (gb * conv).astype(o_ref.dtype)
    v_ref[0:SUBLANES, :] = v_ref[CONV_TM:CONV_TM + SUBLANES, :]


def _conv_in(h, conv_w_in, conv_k, layer):
    nb = D_MODEL // MM_TN
    w_spec = lambda off: pl.BlockSpec((None, D_MODEL, MM_TN), lambda j, i: (layer, 0, j + off))
    return pl.pallas_call(
        _conv_in_kernel,
        grid=(nb, TOKENS // CONV_TM),
        in_specs=[
            pl.BlockSpec((CONV_TM, D_MODEL), lambda j, i: (i, 0)),
            w_spec(0), w_spec(nb), w_spec(2 * nb),
            pl.BlockSpec((None, CONV_W, MM_TN), lambda j, i: (layer, 0, j)),
        ],
        out_specs=pl.BlockSpec((CONV_TM, MM_TN), lambda j, i: (i, j)),
        out_shape=jax.ShapeDtypeStruct((TOKENS, D_MODEL), BF16),
        scratch_shapes=[
            pltpu.VMEM((3, D_MODEL, MM_TN), BF16),
            pltpu.VMEM((CONV_TM + SUBLANES, MM_TN), F32),
        ],
        compiler_params=_params(2),
        name="conv_in",
    )(h, conv_w_in, conv_w_in, conv_w_in, conv_k)


def _ffn_up_kernel(a_ref, w1_ref, w3_ref, o_ref, wb_ref):
    _cast_weights_once((w1_ref, w3_ref), (wb_ref.at[0], wb_ref.at[1]))
    a = a_ref[...]
    p = _dot(a, wb_ref[0])
    q = _dot(a, wb_ref[1])
    o_ref[...] = (_silu(p) * q).astype(o_ref.dtype)


def _ffn_up(h, w13, layer):
    nb = D_FF // MM_TN
    w_spec = lambda off: pl.BlockSpec((None, D_MODEL, MM_TN), lambda j, i: (layer, 0, j + off))
    return pl.pallas_call(
        _ffn_up_kernel,
        grid=(nb, TOKENS // MM_TM),
        in_specs=[pl.BlockSpec((MM_TM, D_MODEL), lambda j, i: (i, 0)), w_spec(0), w_spec(nb)],
        out_specs=pl.BlockSpec((MM_TM, MM_TN), lambda j, i: (i, j)),
        out_shape=jax.ShapeDtypeStruct((TOKENS, D_FF), BF16),
        scratch_shapes=[pltpu.VMEM((2, D_MODEL, MM_TN), BF16)],
        compiler_params=_params(2),
        name="ffn_up",
    )(h, w13, w13)


LOGA_TM = 512


def _loga_kernel(h_ref, wl_ref, wgk_ref, bgk_ref, o_ref):
    low = _dot_nt(h_ref[...], wl_ref[...].astype(BF16))
    z =_dot(low.astype(BF16), wgk_ref[...].astype(BF16)) + bgk_ref[...]
    log_sig = jnp.minimum(z, 0.0) - jnp.log1p(jnp.exp(-jnp.abs(z)))
    o_ref[...] = log_sig * (1.0 / GLA_GATE_NORM)


def _gla_log_decay(h, w_in_t, layer, w_gk, b_gk):
    return pl.pallas_call(
        _loga_kernel,
        grid=(TOKENS // LOGA_TM,),
        in_specs=[
            pl.BlockSpec((LOGA_TM, D_MODEL), lambda i: (i, 0)),
            pl.BlockSpec((None, GLA_GATE_RANK, D_MODEL), lambda i: (layer, GLA_MAIN // GLA_GATE_RANK, 0)),
            pl.BlockSpec((GLA_GATE_RANK, GLA_DK), lambda i: (0, 0)),
            pl.BlockSpec((1, GLA_DK), lambda i: (0, 0)),
        ],
        out_specs=pl.BlockSpec((LOGA_TM, GLA_DK), lambda i: (i, 0)),
        out_shape=jax.ShapeDtypeStruct((TOKENS, GLA_DK), F32),
        compiler_params=_params(1),
        name="gla_log_decay",
    )(h, w_in_t, w_gk, b_gk.reshape(1, GLA_DK))


GLA_ROWS = 512


def _gla_kernel(q_ref, k_ref, v_ref, g_ref, la_ref, ng_ref, o_ref, st_ref):
    @pl.when(pl.program_id(1) == 0)
    def _():
        st_ref[...] = jnp.zeros(st_ref.shape, F32)

    row = lax.broadcasted_iota(jnp.int32, (CHUNK, CHUNK), 0)
    col = lax.broadcasted_iota(jnp.int32, (CHUNK, CHUNK), 1)
    tri = (col <= row).astype(BF16)

    def chunk(ci, carry):
        rows = pl.ds(pl.multiple_of(ci * CHUNK, CHUNK), CHUNK)
        la = la_ref[rows, :]
        la_hi = la.astype(BF16)
        la_lo = (la - la_hi.astype(F32)).astype(BF16)
        bcum = _dot(tri, la_hi) + _dot(tri, la_lo)
        btot = bcum[CHUNK - 1:CHUNK, :]
        k_dec = (k_ref[rows, :].astype(F32) * jnp.exp(btot - bcum)).astype(BF16)
        decay = jnp.exp(btot)
        q = (q_ref[rows, :].astype(F32) * (GLA_DK_HEAD ** -0.5)).astype(BF16)
        for h in range(GLA_HEADS):
            kcols = slice(h * GLA_DK_HEAD, (h + 1) * GLA_DK_HEAD)
            vcols = slice(h * GLA_DV_HEAD, (h + 1) * GLA_DV_HEAD)
            kv_t = lax.dot_general(v_ref[rows, vcols], k_dec[:, kcols],
                                   (((0,), (0,)), ((), ())), preferred_element_type=F32)
            st = st_ref[h] * decay[:, kcols] + kv_t
            st_ref[h] = st
            o = lax.dot_general(q[:, kcols], st.astype(BF16),
                                (((1,), (1,)), ((), ())), preferred_element_type=F32)
            o = o * lax.rsqrt(jnp.mean(o * o, axis=-1, keepdims=True) + EPS) * ng_ref[...]
            o = o * _silu(g_ref[rows, vcols].astype(F32))
            o_ref[rows, vcols] = o.astype(o_ref.dtype)
        return carry

    lax.fori_loop(0, GLA_ROWS // CHUNK, chunk, 0)


def _gla_scan(proj, log_a, norm_g):
    per_seq = SEQ // GLA_ROWS
    rows = lambda width, blk: pl.BlockSpec((GLA_ROWS, width), lambda b, s: (b * per_seq + s, blk))
    return pl.pallas_call(
        _gla_kernel,
        grid=(BATCH, per_seq),
        in_specs=[
            rows(GLA_DK, 0), rows(GLA_DK, 1), rows(GLA_DV, 1), rows(GLA_DV, 2),
            rows(GLA_DK, 0),
            pl.BlockSpec((1, GLA_DV_HEAD), lambda b, s: (0, 0)),
        ],
        out_specs=rows(GLA_DV, 0),
        out_shape=jax.ShapeDtypeStruct((TOKENS, GLA_DV), BF16),
        scratch_shapes=[pltpu.VMEM((GLA_HEADS, GLA_DV_HEAD, GLA_DK_HEAD), F32)],
        compiler_params=_params(2),
        name="gla_scan",
    )(proj, proj, proj, proj, log_a, norm_g.reshape(1, GLA_DV_HEAD))


ROUTE_TM = 512
META_E0, META_E1, META_G0, META_G1, META_R0, META_R1 = range(6)


def _route_kernel(x_ref, g_ref, sh_ref, sc_ref, r_ref, h_ref, meta_ref, cnt_ref, run_ref):
    @pl.when(pl.program_id(0) == 0)
    def _():
        run_ref[...] = jnp.zeros(run_ref.shape, F32)

    h = _norm_mod(x_ref[...], g_ref[...], sc_ref[...], sh_ref[...])
    h_ref[...] = h
    logits =jnp.dot(h, r_ref[...], preferred_element_type=F32, precision=lax.Precision.HIGHEST)
    lane = lax.broadcasted_iota(jnp.int32, logits.shape, 1)
    logits = jnp.where(lane < N_EXPERTS, logits, -jnp.inf)
    m0 = jnp.max(logits, axis=1, keepdims=True)
    e0 = jnp.min(jnp.where(logits == m0, lane, LANES), axis=1, keepdims=True)
    rest = jnp.where(lane == e0, -jnp.inf, logits)
    m1 = jnp.max(rest, axis=1, keepdims=True)
    e1 = jnp.min(jnp.where(rest == m1, lane, LANES), axis=1, keepdims=True)
    p = jnp.exp(m1 - m0)
    gate0 = 1.0 / (1.0 + p)
    gate1 = p / (1.0 + p)

    hot0 = (lane == e0).astype(F32)
    hot1 = (lane == e1).astype(F32)
    both = hot0 + hot1
    row = lax.broadcasted_iota(jnp.int32, (ROUTE_TM, ROUTE_TM), 0)
    col = lax.broadcasted_iota(jnp.int32, (ROUTE_TM, ROUTE_TM), 1)
    before = _dot((col < row).astype(BF16), both.astype(BF16)) + run_ref[0:1, :]
    rank0 = jnp.sum(hot0 * before, axis=1, keepdims=True)
    rank1 = jnp.sum(hot1 * before, axis=1, keepdims=True)
    run_ref[...] = run_ref[...] + jnp.sum(both, axis=0, keepdims=True)
    cnt_ref[...] = run_ref[...]

    meta = jnp.zeros(logits.shape, F32)
    for lane_id, val in ((META_E0, e0.astype(F32)), (META_E1, e1.astype(F32)), (META_G0, gate0),
                         (META_G1, gate1), (META_R0, rank0), (META_R1, rank1)):
        meta = jnp.where(lane == lane_id, val, meta)
    meta_ref[...] = meta


def _route(x, g, sh, sc, router):
    rows, vec, seq_vec = _row_specs(ROUTE_TM)
    router = jnp.pad(router, ((0, 0), (0, LANES - N_EXPERTS)))
    return pl.pallas_call(
        _route_kernel,
        grid=(TOKENS // ROUTE_TM,),
        in_specs=[rows, vec, seq_vec, seq_vec, pl.BlockSpec((D_MODEL, LANES), lambda i: (0, 0))],
        out_specs=[
            rows,
            pl.BlockSpec((ROUTE_TM, LANES), lambda i: (i, 0)),
            pl.BlockSpec((SUBLANES, LANES), lambda i: (0, 0)),
        ],
        out_shape=[
            jax.ShapeDtypeStruct((TOKENS, D_MODEL), F32),
            jax.ShapeDtypeStruct((TOKENS, LANES), F32),
            jax.ShapeDtypeStruct((SUBLANES, LANES), F32),
        ],
        scratch_shapes=[pltpu.VMEM((SUBLANES, LANES), F32)],
        compiler_params=_params(1),
        name="moe_route",
    )(x, g.reshape(1, D_MODEL), sh, sc, router)


def _row_copy(src_hbm, src_row, dst_vmem, dst_row, sem):
    return pltpu.make_async_copy(
        src_hbm.at[pl.ds(src_row, 1), :], dst_vmem.at[pl.ds(dst_row, 1), :], sem)


DMA_UNROLL = 8


def _for_rows(n_rows, fn):
    def body(r0, carry):
        for u in range(DMA_UNROLL):
            fn(r0 * DMA_UNROLL + u, u % 2)
        return carry

    lax.fori_loop(0, n_rows // DMA_UNROLL, body, 0)


def _gather_kernel(tok_ref, h_hbm, o_ref, buf_ref, sem):
    b = pl.program_id(0)

    def issue(blk):
        slot = blk & 1
        _for_rows(MOE_ROWS, lambda r, parity: _row_copy(
            h_hbm, tok_ref[blk * MOE_ROWS + r], buf_ref.at[slot], r, sem.at[slot]).start(priority=parity))

    @pl.when(b == 0)
    def _():
        issue(b)

    @pl.when(b + 1 < pl.num_programs(0))
    def _():
        issue(b + 1)

    slot = b & 1
    _for_rows(MOE_ROWS, lambda r, parity: _row_copy(h_hbm, 0, buf_ref.at[slot], r, sem.at[slot]).wait())
    o_ref[...] = buf_ref[slot].astype(o_ref.dtype)


def _gather_slots(slot_tok, h):
    return pl.pallas_call(
        _gather_kernel,
        grid_spec=pltpu.PrefetchScalarGridSpec(
            num_scalar_prefetch=1,
            grid=(N_SLOT_BLOCKS,),
            in_specs=[pl.BlockSpec(memory_space=pl.ANY)],
            out_specs=pl.BlockSpec((MOE_ROWS, D_MODEL), lambda b, tok: (b, 0)),
            scratch_shapes=[pltpu.VMEM((2, MOE_ROWS, D_MODEL), F32), pltpu.SemaphoreType.DMA((2,))],
        ),
        out_shape=jax.ShapeDtypeStruct((N_SLOTS, D_MODEL), BF16),
        compiler_params=_params(1),
        name="moe_gather",
    )(slot_tok, h)


BLOCK_DMA_PRIORITY = 1


def _grouped_kernel(n_w, compute, first_ref, cnt_ref, a_hbm, *refs):
    w_refs, o_hbm = refs[:n_w], refs[n_w]
    wb_ref, a_buf, o_buf, sem_in, sem_out = refs[n_w + 1:]
    f, e = pl.program_id(0), pl.program_id(1)
    n, first = cnt_ref[e], first_ref[e]
    col = pl.multiple_of(f * MM_TN, MM_TN)

    def copy_in(blk, slot):
        return pltpu.make_async_copy(
            a_hbm.at[pl.ds(blk * MOE_ROWS, MOE_ROWS), :], a_buf.at[slot], sem_in.at[slot])

    def copy_out(blk, slot):
        return pltpu.make_async_copy(
            o_buf.at[slot], o_hbm.at[pl.ds(blk * MOE_ROWS, MOE_ROWS), pl.ds(col, MM_TN)],
            sem_out.at[slot])

    for k in range(n_w):
        wb_ref[k] = w_refs[k][...].astype(BF16)

    @pl.when(jnp.logical_and(jnp.logical_and(f == 0, e == 0), n > 0))
    def _():
        copy_in(first, 0).start(priority=BLOCK_DMA_PRIORITY)

    def block(i, carry):
        slot = i & 1
        copy_in(first + i, slot).wait()

        @pl.when(i + 1 < n)
        def _():
            copy_in(first + i + 1, 1 - slot).start(priority=BLOCK_DMA_PRIORITY)

        @pl.when(i >= 2)
        def _():
            copy_out(first + i - 2, slot).wait()

        o_buf[slot] = compute(a_buf[slot], wb_ref).astype(o_buf.dtype)
        copy_out(first + i, slot).start(priority=BLOCK_DMA_PRIORITY)
        return carry

    lax.fori_loop(0, n, block, 0)

    last_e = e == N_EXPERTS - 1
    e_next = jnp.where(last_e, 0, e + 1)
    is_last_step = jnp.logical_and(last_e, f == pl.num_programs(0) - 1)

    @pl.when(jnp.logical_and(jnp.logical_not(is_last_step), cnt_ref[e_next] > 0))
    def _():
        copy_in(first_ref[e_next], 0).start(priority=BLOCK_DMA_PRIORITY)

    @pl.when(n >= 2)
    def _():
        copy_out(first + n - 2, n & 1).wait()

    @pl.when(n >= 1)
    def _():
        copy_out(first + n - 1, (n - 1) & 1).wait()

    @pl.when(last_e)
    def _():
        o_buf[0] = jnp.zeros(o_buf.shape[1:], o_buf.dtype)

        def zero_block(blk, carry):
            copy_out(blk, 0).start()
            copy_out(blk, 0).wait()
            return carry

        lax.fori_loop(first + n, N_SLOT_BLOCKS, zero_block, 0)


def _grouped_matmul(name, compute, first_blk, n_blk, a, w4, layer, col_offsets, n_out, out_dtype):
    k = a.shape[1]
    w_spec = lambda off: pl.BlockSpec(
        (None, None, k, MM_TN), lambda f, e, first, cnt: (layer, e, 0, f + off // MM_TN))
    n_w = len(col_offsets)
    return pl.pallas_call(
        functools.partial(_grouped_kernel, n_w, compute),
        grid_spec=pltpu.PrefetchScalarGridSpec(
            num_scalar_prefetch=2,
            grid=(n_out // MM_TN, N_EXPERTS),
            in_specs=[pl.BlockSpec(memory_space=pl.ANY)] + [w_spec(off) for off in col_offsets],
            out_specs=pl.BlockSpec(memory_space=pl.ANY),
            scratch_shapes=[
                pltpu.VMEM((n_w, k, MM_TN), BF16),
                pltpu.VMEM((2, MOE_ROWS, k), BF16),
                pltpu.VMEM((2, MOE_ROWS, MM_TN), out_dtype),
                pltpu.SemaphoreType.DMA((2,)),
                pltpu.SemaphoreType.DMA((2,)),
            ],
        ),
        out_shape=jax.ShapeDtypeStruct((N_SLOTS, n_out), out_dtype),
        compiler_params=_params(2),
        name=name,
    )(first_blk, n_blk, a, *([w4] * n_w))


def _swiglu_tile(a, wb_ref):
    return _silu(_dot(a, wb_ref[0])) * _dot(a, wb_ref[1])


def _down_tile(a, wb_ref):
    return _dot(a, wb_ref[0])


COMB_TM = 256


def _combine_kernel(final, dest_ref, x_ref, meta_ref, g2_ref, fg_ref, y_hbm, o_ref, buf_ref, sem):
    i = pl.program_id(0)

    def issue(tile):
        slot = tile & 1
        for k in range(TOP_K):
            _for_rows(COMB_TM, lambda r, parity: _row_copy(
                y_hbm, dest_ref[TOP_K * (tile * COMB_TM + r) + k], buf_ref.at[slot, k], r,
                sem.at[slot]).start(priority=BLOCK_DMA_PRIORITY))

    @pl.when(i == 0)
    def _():
        issue(i)

    @pl.when(i + 1 < pl.num_programs(0))
    def _():
        issue(i + 1)

    slot = i & 1
    for k in range(TOP_K):
        _for_rows(COMB_TM, lambda r, parity: _row_copy(
            y_hbm, 0, buf_ref.at[slot, k], r, sem.at[slot]).wait())
    meta = meta_ref[...]
    f = meta[:, META_G0:META_G0 + 1] * buf_ref[slot, 0] + meta[:, META_G1:META_G1 + 1] * buf_ref[slot, 1]
    x_new = x_ref[...] + g2_ref[...] * f
    if final:
        ms = jnp.mean(x_new * x_new, axis=-1, keepdims=True)
        x_new = x_new * lax.rsqrt(ms + EPS) * fg_ref[...]
    o_ref[...] = x_new


def _combine(dest_flat, x, meta, g2, y_slots, final_g, final):
    per_seq = SEQ // COMB_TM
    return pl.pallas_call(
        functools.partial(_combine_kernel, final),
        grid_spec=pltpu.PrefetchScalarGridSpec(
            num_scalar_prefetch=1,
            grid=(TOKENS // COMB_TM,),
            in_specs=[
                pl.BlockSpec((COMB_TM, D_MODEL), lambda i, d: (i, 0)),
                pl.BlockSpec((COMB_TM, LANES), lambda i, d: (i, 0)),
                pl.BlockSpec((None, 1, D_MODEL), lambda i, d: (i // per_seq, 0, 0)),
                pl.BlockSpec((1, D_MODEL), lambda i, d: (0, 0)),
                pl.BlockSpec(memory_space=pl.ANY),
            ],
            out_specs=pl.BlockSpec((COMB_TM, D_MODEL), lambda i, d: (i, 0)),
            scratch_shapes=[pltpu.VMEM((2, TOP_K, COMB_TM, D_MODEL), F32),
                            pltpu.SemaphoreType.DMA((2,))],
        ),
        out_shape=jax.ShapeDtypeStruct((TOKENS, D_MODEL), F32),
        compiler_params=_params(1),
        name="moe_combine",
    )(dest_flat, x, meta, g2, final_g.reshape(1, D_MODEL), y_slots)


def _slot_plan(meta, counts):
    top_e = meta[:, META_E0:META_E1 + 1].astype(jnp.int32)
    rank = meta[:, META_R0:META_R1 + 1].astype(jnp.int32)
    counts = counts[0, :N_EXPERTS].astype(jnp.int32)
    n_blk = (counts + MOE_ROWS - 1) // MOE_ROWS
    first_blk = jnp.cumsum(n_blk) - n_blk
    dest = (first_blk[top_e] * MOE_ROWS + rank).reshape(TOKENS * TOP_K)
    pair_tok = jnp.arange(TOKENS * TOP_K, dtype=jnp.int32) // TOP_K
    slot_tok = jnp.zeros((N_SLOTS,), jnp.int32).at[dest].set(pair_tok)
    return dest, slot_tok, first_blk.astype(jnp.int32), n_blk.astype(jnp.int32)


def kernel(x, c, ada_w, ada_b, norm_g, conv_w_in, conv_k, conv_w_out, gla_w_in, gla_w_gk, gla_b_gk,
           gla_norm_g, gla_w_out, ffn_w13, ffn_w2, moe_router, moe_w13, moe_w2, final_g):
    assert x.shape == (BATCH, SEQ, D_MODEL) and x.dtype == F32
    mod = _ada_all(c, ada_w, ada_b)[:, :BATCH]
    xt = x.reshape(TOKENS, D_MODEL)
    gla_w_in_t = jnp.swapaxes(gla_w_in, 1, 2)
    for i in range(DEPTH):
        j = i // 2
        sh1, sc1, g1, sh2, sc2, g2 = (
            mod[i, :, n * D_MODEL:(n + 1) * D_MODEL].reshape(BATCH, 1, D_MODEL) for n in range(6))
        h = _norm_modulate(xt, norm_g[i, 0], sh1, sc1)
        if i % 2 == 0:
            y = _conv_in(h, conv_w_in, conv_k, j)
            xt = _matmul_residual(y, conv_w_out, j, xt, g1)
            h = _norm_modulate(xt, norm_g[i, 1], sh2, sc2)
            t = _ffn_up(h, ffn_w13, j)
            xt = _matmul_residual(t, ffn_w2, j, xt, g2)
        else:
            proj = _matmul_nt(h, gla_w_in_t, j, GLA_MAIN, BF16)
            log_a = _gla_log_decay(h, gla_w_in_t, j, gla_w_gk[j], gla_b_gk[j])
            o = _gla_scan(proj, log_a, gla_norm_g[j])
            xt = _matmul_residual(o, gla_w_out, j, xt, g1)
            h32, meta, counts = _route(xt, norm_g[i, 1], sh2, sc2, moe_router[j])
            dest, slot_tok, first_blk, n_blk = _slot_plan(meta, counts)
            xs = _gather_slots(slot_tok, h32)
            t = _grouped_matmul("moe_up", _swiglu_tile, first_blk, n_blk, xs, moe_w13, j,
                                (0, D_FF), D_FF, BF16)
            y = _grouped_matmul("moe_down", _down_tile, first_blk, n_blk, t, moe_w2, j,
                                (0,), D_MODEL, F32)
            xt = _combine(dest, xt, meta, g2, y, final_g, final=(i == DEPTH - 1))
    return xt.reshape(BATCH, SEQ, D_MODEL)
```

```python
import functools

import jax
import jax.numpy as jnp
from jax import lax
from jax.experimental import pallas as pl
from jax.experimental.pallas import tpu as pltpu

D_MODEL = 2048
BATCH = 4
SEQ = 2048
TOKENS = BATCH * SEQ
DEPTH = 4
CHUNK = 64
EPS = 1e-6
CONV_W = 3
GLA_HEADS = 4
GLA_DK = D_MODEL // 2
GLA_DV = D_MODEL
GLA_DK_HEAD = GLA_DK // GLA_HEADS
GLA_DV_HEAD = GLA_DV // GLA_HEADS
GLA_GATE_RANK = 16
GLA_GATE_NORM = 16.0
GLA_MAIN = 2 * GLA_DK + 2 * GLA_DV
D_FF = 5632
N_EXPERTS = 8
TOP_K = 2

LANES = 128
SUBLANES = 8
MXU_DIM = 256
VMEM_LIMIT = 56 * 1024 * 1024

MOE_ROWS = 256
N_SLOT_BLOCKS = TOKENS * TOP_K // MOE_ROWS + N_EXPERTS
N_SLOTS = N_SLOT_BLOCKS * MOE_ROWS

F32 = jnp.float32
BF16 = jnp.bfloat16


def _params(n_axes):
    return pltpu.CompilerParams(
        dimension_semantics=("arbitrary",) * n_axes, vmem_limit_bytes=VMEM_LIMIT)


def _dot(a, b):
    return jnp.dot(a, b, preferred_element_type=F32)


def _silu(v):
    return v * jax.nn.sigmoid(v)


ADA_TN = 1024


def _ada_kernel(c_ref, w_ref, b_ref, o_ref):
    c_act = _silu(c_ref[...]).astype(BF16)
    o_ref[...] = _dot(c_act, w_ref[...].astype(BF16)) + b_ref[...]


def _ada_all(c, ada_w, ada_b):
    c_pad = jnp.pad(c, ((0, SUBLANES - BATCH), (0, 0)))
    n = 6 * D_MODEL
    return pl.pallas_call(
        _ada_kernel,
        grid=(DEPTH, n // ADA_TN),
        in_specs=[
            pl.BlockSpec((SUBLANES, D_MODEL), lambda l, j: (0, 0)),
            pl.BlockSpec((None, D_MODEL, ADA_TN), lambda l, j: (l, 0, j)),
            pl.BlockSpec((None, 1, ADA_TN), lambda l, j: (l, 0, j)),
        ],
        out_specs=pl.BlockSpec((None, SUBLANES, ADA_TN), lambda l, j: (l, 0, j)),
        out_shape=jax.ShapeDtypeStruct((DEPTH, SUBLANES, n), F32),
        compiler_params=_params(2),
        name="ada_mod",
    )(c_pad, ada_w, ada_b.reshape(DEPTH, 1, n))


NORM_TM = 512


def _norm_mod(x, g, sc, sh):
    ms = jnp.mean(x * x, axis=-1, keepdims=True)
    y = x * lax.rsqrt(ms + EPS) * g
    return y * (1.0 + sc) + sh


def _norm_mod_kernel(x_ref, g_ref, sh_ref, sc_ref, h_ref):
    h_ref[...] = _norm_mod(x_ref[...], g_ref[...], sc_ref[...], sh_ref[...]).astype(h_ref.dtype)


def _row_specs(tm):
    per_seq = SEQ // tm
    rows = pl.BlockSpec((tm, D_MODEL), lambda i: (i, 0))
    vec = pl.BlockSpec((1, D_MODEL), lambda i: (0, 0))
    seq_vec = pl.BlockSpec((None, 1, D_MODEL), lambda i: (i // per_seq, 0, 0))
    return rows, vec, seq_vec


def _norm_modulate(x, g, sh, sc):
    rows, vec, seq_vec = _row_specs(NORM_TM)
    return pl.pallas_call(
        _norm_mod_kernel,
        grid=(TOKENS // NORM_TM,),
        in_specs=[rows, vec, seq_vec, seq_vec],
        out_specs=rows,
        out_shape=jax.ShapeDtypeStruct((TOKENS, D_MODEL), BF16),
        compiler_params=_params(1),
        name="norm_mod",
    )(x, g.reshape(1, D_MODEL), sh, sc)


MM_TM = 1024
MM_TN = 512


def _cast_weights_once(w_refs, wb_refs):
    @pl.when(pl.program_id(1) == 0)
    def _():
        for w_ref, wb_ref in zip(w_refs, wb_refs):
            wb_ref[...] = w_ref[...].astype(BF16)


def _dot_nt(a, b_t):
    return lax.dot_general(a, b_t, (((1,), (1,)), ((), ())), preferred_element_type=F32)


def _mm_nt_kernel(a_ref, wt_ref, o_ref, wb_ref):
    _cast_weights_once((wt_ref,), (wb_ref,))
    o_ref[...] = _dot_nt(a_ref[...], wb_ref[...]).astype(o_ref.dtype)


def _matmul_nt(a, wt3, layer, n_out, out_dtype):
    k = a.shape[1]
    tn = _wide_col_tile(k)
    return pl.pallas_call(
        _mm_nt_kernel,
        grid=(n_out // tn, TOKENS // MM_TM),
        in_specs=[
            pl.BlockSpec((MM_TM, k), lambda j, i: (i, 0)),
            pl.BlockSpec((None, tn, k), lambda j, i: (layer, j, 0)),
        ],
        out_specs=pl.BlockSpec((MM_TM, tn), lambda j, i: (i, j)),
        out_shape=jax.ShapeDtypeStruct((TOKENS, n_out), out_dtype),
        scratch_shapes=[pltpu.VMEM((tn, k), BF16)],
        compiler_params=_params(2),
        name="matmul_nt",
    )(a, wt3)


RES_ROW_BYTES = 6 * 1024 * 1024
WIDE_TILE_BYTES = 12 * 1024 * 1024


def _wide_col_tile(k):
    return min(2 * MM_TN, WIDE_TILE_BYTES // (4 * k) // MXU_DIM * MXU_DIM)


def _mm_res_kernel(a_ref, w_ref, x_ref, g_ref, o_ref, wb_ref):
    _cast_weights_once((w_ref,), (wb_ref,))
    o_ref[...] = x_ref[...] + g_ref[...] * _dot(a_ref[...], wb_ref[...])


def _matmul_residual(a, w3, layer, x, gate):
    k = a.shape[1]
    tm = min(MM_TM, RES_ROW_BYTES // (2 * k) // MXU_DIM * MXU_DIM)
    tn = _wide_col_tile(k)
    per_seq = SEQ // tm
    return pl.pallas_call(
        _mm_res_kernel,
        grid=(D_MODEL // tn, TOKENS // tm),
        in_specs=[
            pl.BlockSpec((tm, k), lambda j, i: (i, 0)),
            pl.BlockSpec((None, k, tn), lambda j, i: (layer, 0, j)),
            pl.BlockSpec((tm, tn), lambda j, i: (i, j)),
            pl.BlockSpec((None, 1, tn), lambda j, i: (i // per_seq, 0, j)),
        ],
        out_specs=pl.BlockSpec((tm, tn), lambda j, i: (i, j)),
        out_shape=jax.ShapeDtypeStruct((TOKENS, D_MODEL), F32),
        scratch_shapes=[pltpu.VMEM((k, tn), BF16)],
        compiler_params=_params(2),
        name="matmul_residual",
    )(a, w3, x, gate)


CONV_TM = 1024


def _conv_in_kernel(a_ref, wgb_ref, wgc_ref, wu_ref, kc_ref, o_ref, wb_ref, v_ref):
    i = pl.program_id(1)
    _cast_weights_once((wgb_ref, wgc_ref, wu_ref), (wb_ref.at[0], wb_ref.at[1], wb_ref.at[2]))

    @pl.when(i % (SEQ // CONV_TM) == 0)
    def _():
        v_ref[0:SUBLANES, :] = jnp.zeros((SUBLANES, MM_TN), F32)

    a = a_ref[...]
    gc = _dot(a, wb_ref[1])
    u = _dot(a, wb_ref[2])
    v_ref[SUBLANES:SUBLANES + CONV_TM, :] = gc * u
    conv = kc_ref[2:3, :] * v_ref[SUBLANES:SUBLANES + CONV_TM, :]
    conv = conv + kc_ref[1:2, :] * v_ref[SUBLANES - 1:SUBLANES - 1 + CONV_TM, :]
    conv = conv + kc_ref[0:1, :] * v_ref[SUBLANES - 2:SUBLANES - 2 + CONV_TM, :]
    gb = _dot(a, wb_ref[0])
    o_ref[...] = (gb * conv).astype(o_ref.dtype)
    v_ref[0:SUBLANES, :] = v_ref[CONV_TM:CONV_TM + SUBLANES, :]


def _conv_in(h, conv_w_in, conv_k, layer):
    nb = D_MODEL // MM_TN
    w_spec = lambda off: pl.BlockSpec((None, D_MODEL, MM_TN), lambda j, i: (layer, 0, j + off))
    return pl.pallas_call(
        _conv_in_kernel,
        grid=(nb, TOKENS // CONV_TM),
        in_specs=[
            pl.BlockSpec((CONV_TM, D_MODEL), lambda j, i: (i, 0)),
            w_spec(0), w_spec(nb), w_spec(2 * nb),
            pl.BlockSpec((None, CONV_W, MM_TN), lambda j, i: (layer, 0, j)),
        ],
        out_specs=pl.BlockSpec((CONV_TM, MM_TN), lambda j, i: (i, j)),
        out_shape=jax.ShapeDtypeStruct((TOKENS, D_MODEL), BF16),
        scratch_shapes=[
            pltpu.VMEM((3, D_MODEL, MM_TN), BF16),
            pltpu.VMEM((CONV_TM + SUBLANES, MM_TN), F32),
        ],
        compiler_params=_params(2),
        name="conv_in",
    )(h, conv_w_in, conv_w_in, conv_w_in, conv_k)


def _ffn_up_kernel(a_ref, w1_ref, w3_ref, o_ref, wb_ref):
    _cast_weights_once((w1_ref, w3_ref), (wb_ref.at[0], wb_ref.at[1]))
    a = a_ref[...]
    p = _dot(a, wb_ref[0])
    q = _dot(a, wb_ref[1])
    o_ref[...] = (_silu(p) * q).astype(o_ref.dtype)


def _ffn_up(h, w13, layer):
    nb = D_FF // MM_TN
    w_spec = lambda off: pl.BlockSpec((None, D_MODEL, MM_TN), lambda j, i: (layer, 0, j + off))
    return pl.pallas_call(
        _ffn_up_kernel,
        grid=(nb, TOKENS // MM_TM),
        in_specs=[pl.BlockSpec((MM_TM, D_MODEL), lambda j, i: (i, 0)), w_spec(0), w_spec(nb)],
        out_specs=pl.BlockSpec((MM_TM, MM_TN), lambda j, i: (i, j)),
        out_shape=jax.ShapeDtypeStruct((TOKENS, D_FF), BF16),
        scratch_shapes=[pltpu.VMEM((2, D_MODEL, MM_TN), BF16)],
        compiler_params=_params(2),
        name="ffn_up",
    )(h, w13, w13)


LOGA_TM = 512


def _loga_kernel(h_ref, wl_ref, wgk_ref, bgk_ref, o_ref):
    low = _dot_nt(h_ref[...], wl_ref[...].astype(BF16))
    z =_dot(low.astype(BF16), wgk_ref[...].astype(BF16)) + bgk_ref[...]
    log_sig = jnp.minimum(z, 0.0) - jnp.log1p(jnp.exp(-jnp.abs(z)))
    o_ref[...] = log_sig * (1.0 / GLA_GATE_NORM)


def _gla_log_decay(h, w_in_t, layer, w_gk, b_gk):
    return pl.pallas_call(
        _loga_kernel,
        grid=(TOKENS // LOGA_TM,),
        in_specs=[
            pl.BlockSpec((LOGA_TM, D_MODEL), lambda i: (i, 0)),
            pl.BlockSpec((None, GLA_GATE_RANK, D_MODEL), lambda i: (layer, GLA_MAIN // GLA_GATE_RANK, 0)),
            pl.BlockSpec((GLA_GATE_RANK, GLA_DK), lambda i: (0, 0)),
            pl.BlockSpec((1, GLA_DK), lambda i: (0, 0)),
        ],
        out_specs=pl.BlockSpec((LOGA_TM, GLA_DK), lambda i: (i, 0)),
        out_shape=jax.ShapeDtypeStruct((TOKENS, GLA_DK), F32),
        compiler_params=_params(1),
        name="gla_log_decay",
    )(h, w_in_t, w_gk, b_gk.reshape(1, GLA_DK))


GLA_ROWS = 512


def _gla_kernel(q_ref, k_ref, v_ref, g_ref, la_ref, ng_ref, o_ref, st_ref):
    @pl.when(pl.program_id(1) == 0)
    def _():
        st_ref[...] = jnp.zeros(st_ref.shape, F32)

    row = lax.broadcasted_iota(jnp.int32, (CHUNK, CHUNK), 0)
    col = lax.broadcasted_iota(jnp.int32, (CHUNK, CHUNK), 1)
    tri = (col <= row).astype(BF16)

    def chunk(ci, carry):
        rows = pl.ds(pl.multiple_of(ci * CHUNK, CHUNK), CHUNK)
        la = la_ref[rows, :]
        la_hi = la.astype(BF16)
        la_lo = (la - la_hi.astype(F32)).astype(BF16)
        bcum = _dot(tri, la_hi) + _dot(tri, la_lo)
        btot = bcum[CHUNK - 1:CHUNK, :]
        k_dec = (k_ref[rows, :].astype(F32) * jnp.exp(btot - bcum)).astype(BF16)
        decay = jnp.exp(btot)
        q = (q_ref[rows, :].astype(F32) * (GLA_DK_HEAD ** -0.5)).astype(BF16)
        for h in range(GLA_HEADS):
            kcols = slice(h * GLA_DK_HEAD, (h + 1) * GLA_DK_HEAD)
            vcols = slice(h * GLA_DV_HEAD, (h + 1) * GLA_DV_HEAD)
            kv_t = lax.dot_general(v_ref[rows, vcols], k_dec[:, kcols],
                                   (((0,), (0,)), ((), ())), preferred_element_type=F32)
            st = st_ref[h] * decay[:, kcols] + kv_t
            st_ref[h] = st
            o = lax.dot_general(q[:, kcols], st.astype(BF16),
                                (((1,), (1,)), ((), ())), preferred_element_type=F32)
            o = o * lax.rsqrt(jnp.mean(o * o, axis=-1, keepdims=True) + EPS) * ng_ref[...]
            o = o * _silu(g_ref[rows, vcols].astype(F32))
            o_ref[rows, vcols] = o.astype(o_ref.dtype)
        return carry

    lax.fori_loop(0, GLA_ROWS // CHUNK, chunk, 0)


def _gla_scan(proj, log_a, norm_g):
    per_seq = SEQ // GLA_ROWS
    rows = lambda width, blk: pl.BlockSpec((GLA_ROWS, width), lambda b, s: (b * per_seq + s, blk))
    return pl.pallas_call(
        _gla_kernel,
        grid=(BATCH, per_seq),
        in_specs=[
            rows(GLA_DK, 0), rows(GLA_DK, 1), rows(GLA_DV, 1), rows(GLA_DV, 2),
            rows(GLA_DK, 0),
            pl.BlockSpec((1, GLA_DV_HEAD), lambda b, s: (0, 0)),
        ],
        out_specs=rows(GLA_DV, 0),
        out_shape=jax.ShapeDtypeStruct((TOKENS, GLA_DV), BF16),
        scratch_shapes=[pltpu.VMEM((GLA_HEADS, GLA_DV_HEAD, GLA_DK_HEAD), F32)],
        compiler_params=_params(2),
        name="gla_scan",
    )(proj, proj, proj, proj, log_a, norm_g.reshape(1, GLA_DV_HEAD))


ROUTE_TM = 512
META_E0, META_E1, META_G0, META_G1, META_R0, META_R1 = range(6)


def _route_kernel(x_ref, g_ref, sh_ref, sc_ref, r_ref, h_ref, meta_ref, cnt_ref, run_ref):
    @pl.when(pl.program_id(0) == 0)
    def _():
        run_ref[...] = jnp.zeros(run_ref.shape, F32)

    h = _norm_mod(x_ref[...], g_ref[...], sc_ref[...], sh_ref[...])
    h_ref[...] = h
    logits =jnp.dot(h, r_ref[...], preferred_element_type=F32, precision=lax.Precision.HIGHEST)
    lane = lax.broadcasted_iota(jnp.int32, logits.shape, 1)
    logits = jnp.where(lane < N_EXPERTS, logits, -jnp.inf)
    m0 = jnp.max(logits, axis=1, keepdims=True)
    e0 = jnp.min(jnp.where(logits == m0, lane, LANES), axis=1, keepdims=True)
    rest = jnp.where(lane == e0, -jnp.inf, logits)
    m1 = jnp.max(rest, axis=1, keepdims=True)
    e1 = jnp.min(jnp.where(rest == m1, lane, LANES), axis=1, keepdims=True)
    p = jnp.exp(m1 - m0)
    gate0 = 1.0 / (1.0 + p)
    gate1 = p / (1.0 + p)

    hot0 = (lane == e0).astype(F32)
    hot1 = (lane == e1).astype(F32)
    both = hot0 + hot1
    row = lax.broadcasted_iota(jnp.int32, (ROUTE_TM, ROUTE_TM), 0)
    col = lax.broadcasted_iota(jnp.int32, (ROUTE_TM, ROUTE_TM), 1)
    before = _dot((col < row).astype(BF16), both.astype(BF16)) + run_ref[0:1, :]
    rank0 = jnp.sum(hot0 * before, axis=1, keepdims=True)
    rank1 = jnp.sum(hot1 * before, axis=1, keepdims=True)
    run_ref[...] = run_ref[...] + jnp.sum(both, axis=0, keepdims=True)
    cnt_ref[...] = run_ref[...]

    meta = jnp.zeros(logits.shape, F32)
    for lane_id, val in ((META_E0, e0.astype(F32)), (META_E1, e1.astype(F32)), (META_G0, gate0),
                         (META_G1, gate1), (META_R0, rank0), (META_R1, rank1)):
        meta = jnp.where(lane == lane_id, val, meta)
    meta_ref[...] = meta


def _route(x, g, sh, sc, router):
    rows, vec, seq_vec = _row_specs(ROUTE_TM)
    router = jnp.pad(router, ((0, 0), (0, LANES - N_EXPERTS)))
    return pl.pallas_call(
        _route_kernel,
        grid=(TOKENS // ROUTE_TM,),
        in_specs=[rows, vec, seq_vec, seq_vec, pl.BlockSpec((D_MODEL, LANES), lambda i: (0, 0))],
        out_specs=[
            rows,
            pl.BlockSpec((ROUTE_TM, LANES), lambda i: (i, 0)),
            pl.BlockSpec((SUBLANES, LANES), lambda i: (0, 0)),
        ],
        out_shape=[
            jax.ShapeDtypeStruct((TOKENS, D_MODEL), F32),
            jax.ShapeDtypeStruct((TOKENS, LANES), F32),
            jax.ShapeDtypeStruct((SUBLANES, LANES), F32),
        ],
        scratch_shapes=[pltpu.VMEM((SUBLANES, LANES), F32)],
        compiler_params=_params(1),
        name="moe_route",
    )(x, g.reshape(1, D_MODEL), sh, sc, router)


def _row_copy(src_hbm, src_row, dst_vmem, dst_row, sem):
    return pltpu.make_async_copy(
        src_hbm.at[pl.ds(src_row, 1), :], dst_vmem.at[pl.ds(dst_row, 1), :], sem)


DMA_UNROLL = 8


def _for_rows(n_rows, fn):
    def body(r0, carry):
        for u in range(DMA_UNROLL):
            fn(r0 * DMA_UNROLL + u, u % 2)
        return carry

    lax.fori_loop(0, n_rows // DMA_UNROLL, body, 0)


def _gather_kernel(tok_ref, h_hbm, o_ref, buf_ref, sem):
    b = pl.program_id(0)

    def issue(blk):
        slot = blk & 1
        _for_rows(MOE_ROWS, lambda r, parity: _row_copy(
            h_hbm, tok_ref[blk * MOE_ROWS + r], buf_ref.at[slot], r, sem.at[slot]).start())

    @pl.when(b == 0)
    def _():
        issue(b)

    @pl.when(b + 1 < pl.num_programs(0))
    def _():
        issue(b + 1)

    slot = b & 1
    _for_rows(MOE_ROWS, lambda r, parity: _row_copy(h_hbm, 0, buf_ref.at[slot], r, sem.at[slot]).wait())
    o_ref[...] = buf_ref[slot].astype(o_ref.dtype)


def _gather_slots(slot_tok, h):
    return pl.pallas_call(
        _gather_kernel,
        grid_spec=pltpu.PrefetchScalarGridSpec(
            num_scalar_prefetch=1,
            grid=(N_SLOT_BLOCKS,),
            in_specs=[pl.BlockSpec(memory_space=pl.ANY)],
            out_specs=pl.BlockSpec((MOE_ROWS, D_MODEL), lambda b, tok: (b, 0)),
            scratch_shapes=[pltpu.VMEM((2, MOE_ROWS, D_MODEL), F32), pltpu.SemaphoreType.DMA((2,))],
        ),
        out_shape=jax.ShapeDtypeStruct((N_SLOTS, D_MODEL), BF16),
        compiler_params=_params(1),
        name="moe_gather",
    )(slot_tok, h)


BLOCK_DMA_PRIORITY = 1
CHUNK_BLOCKS = 2


def _grouped_kernel(n_w, compute, first_ref, cnt_ref, a_hbm, *refs):
    w_refs, o_hbm = refs[:n_w], refs[n_w]
    wb_ref, a_buf, o_buf, sem_in, sem_out = refs[n_w + 1:]
    f, e = pl.program_id(0), pl.program_id(1)
    n, first = cnt_ref[e], first_ref[e]
    col = pl.multiple_of(f * MM_TN, MM_TN)
    n_full = n // CHUNK_BLOCKS
    has_tail = n % CHUNK_BLOCKS == 1
    tail_blk = first + CHUNK_BLOCKS * n_full
    tail_slot = n_full & 1

    def copy_in(blk, n_blocks, slot):
        rows = n_blocks * MOE_ROWS
        return pltpu.make_async_copy(
            a_hbm.at[pl.ds(blk * MOE_ROWS, rows), :], a_buf.at[slot, pl.ds(0, rows), :], sem_in.at[slot])

    def copy_out(blk, n_blocks, slot):
        rows = n_blocks * MOE_ROWS
        return pltpu.make_async_copy(
            o_buf.at[slot, pl.ds(0, rows), :],
            o_hbm.at[pl.ds(blk * MOE_ROWS, rows), pl.ds(col, MM_TN)], sem_out.at[slot])

    def start_first_chunk(expert):
        @pl.when(cnt_ref[expert] >= CHUNK_BLOCKS)
        def _():
            copy_in(first_ref[expert], CHUNK_BLOCKS, 0).start(priority=BLOCK_DMA_PRIORITY)

        @pl.when(cnt_ref[expert] == 1)
        def _():
            copy_in(first_ref[expert], 1, 0).start(priority=BLOCK_DMA_PRIORITY)

    for k in range(n_w):
        wb_ref[k] = w_refs[k][...].astype(BF16)

    @pl.when(jnp.logical_and(f == 0, e == 0))
    def _():
        start_first_chunk(e)

    def full_chunk(c, carry):
        slot = c & 1
        blk = first + CHUNK_BLOCKS * c
        copy_in(blk, CHUNK_BLOCKS, slot).wait()

        @pl.when(c + 1 < n_full)
        def _():
            copy_in(blk + CHUNK_BLOCKS, CHUNK_BLOCKS, 1 - slot).start(priority=BLOCK_DMA_PRIORITY)

        @pl.when(jnp.logical_and(c + 1 == n_full, has_tail))
        def _():
            copy_in(blk + CHUNK_BLOCKS, 1, 1 - slot).start(priority=BLOCK_DMA_PRIORITY)

        @pl.when(c >= 2)
        def _():
            copy_out(blk - 2 * CHUNK_BLOCKS, CHUNK_BLOCKS, slot).wait()

        o_buf[slot] = compute(a_buf[slot], wb_ref).astype(o_buf.dtype)
        copy_out(blk, CHUNK_BLOCKS, slot).start(priority=BLOCK_DMA_PRIORITY)
        return carry

    lax.fori_loop(0, n_full, full_chunk, 0)

    @pl.when(has_tail)
    def _():
        copy_in(tail_blk, 1, tail_slot).wait()

        @pl.when(n_full >= 2)
        def _():
            copy_out(tail_blk - 2 * CHUNK_BLOCKS, CHUNK_BLOCKS, tail_slot).wait()

        o_buf[tail_slot, 0:MOE_ROWS, :] = compute(
            a_buf[tail_slot, 0:MOE_ROWS, :], wb_ref).astype(o_buf.dtype)
        copy_out(tail_blk, 1, tail_slot).start(priority=BLOCK_DMA_PRIORITY)

    last_e = e == N_EXPERTS - 1
    is_last_step = jnp.logical_and(last_e, f == pl.num_programs(0) - 1)

    @pl.when(jnp.logical_not(is_last_step))
    def _():
        start_first_chunk(jnp.where(last_e, 0, e + 1))

    @pl.when(jnp.logical_and(n_full >= 2, jnp.logical_not(has_tail)))
    def _():
        copy_out(tail_blk - 2 * CHUNK_BLOCKS, CHUNK_BLOCKS, tail_slot).wait()

    @pl.when(n_full >= 1)
    def _():
        copy_out(tail_blk - CHUNK_BLOCKS, CHUNK_BLOCKS, 1 - tail_slot).wait()

    @pl.when(has_tail)
    def _():
        copy_out(tail_blk, 1, tail_slot).wait()

    @pl.when(last_e)
    def _():
        o_buf[0, 0:MOE_ROWS, :] = jnp.zeros((MOE_ROWS, MM_TN), o_buf.dtype)

        def zero_block(blk, carry):
            copy_out(blk, 1, 0).start()
            copy_out(blk, 1, 0).wait()
            return carry

        lax.fori_loop(first + n, N_SLOT_BLOCKS, zero_block, 0)


def _grouped_matmul(name, compute, first_blk, n_blk, a, w4, layer, col_offsets, n_out, out_dtype):
    k = a.shape[1]
    w_spec = lambda off: pl.BlockSpec(
        (None, None, k, MM_TN), lambda f, e, first, cnt: (layer, e, 0, f + off // MM_TN))
    n_w = len(col_offsets)
    return pl.pallas_call(
        functools.partial(_grouped_kernel, n_w, compute),
        grid_spec=pltpu.PrefetchScalarGridSpec(
            num_scalar_prefetch=2,
            grid=(n_out // MM_TN, N_EXPERTS),
            in_specs=[pl.BlockSpec(memory_space=pl.ANY)] + [w_spec(off) for off in col_offsets],
            out_specs=pl.BlockSpec(memory_space=pl.ANY),
            scratch_shapes=[
                pltpu.VMEM((n_w, k, MM_TN), BF16),
                pltpu.VMEM((2, CHUNK_BLOCKS * MOE_ROWS, k), BF16),
                pltpu.VMEM((2, CHUNK_BLOCKS * MOE_ROWS, MM_TN), out_dtype),
                pltpu.SemaphoreType.DMA((2,)),
                pltpu.SemaphoreType.DMA((2,)),
            ],
        ),
        out_shape=jax.ShapeDtypeStruct((N_SLOTS, n_out), out_dtype),
        compiler_params=_params(2),
        name=name,
    )(first_blk, n_blk, a, *([w4] * n_w))


def _swiglu_tile(a, wb_ref):
    return _silu(_dot(a, wb_ref[0])) * _dot(a, wb_ref[1])


def _down_tile(a, wb_ref):
    return _dot(a, wb_ref[0])


COMB_TM = 256


def _combine_kernel(final, dest_ref, x_ref, meta_ref, g2_ref, fg_ref, y_hbm, o_ref, buf_ref, sem):
    i = pl.program_id(0)

    def issue(tile):
        slot = tile & 1
        for k in range(TOP_K):
            _for_rows(COMB_TM, lambda r, parity: _row_copy(
                y_hbm, dest_ref[TOP_K * (tile * COMB_TM + r) + k], buf_ref.at[slot, k], r,
                sem.at[slot]).start(priority=BLOCK_DMA_PRIORITY))

    @pl.when(i == 0)
    def _():
        issue(i)

    @pl.when(i + 1 < pl.num_programs(0))
    def _():
        issue(i + 1)

    slot = i & 1
    for k in range(TOP_K):
        _for_rows(COMB_TM, lambda r, parity: _row_copy(
            y_hbm, 0, buf_ref.at[slot, k], r, sem.at[slot]).wait())
    meta = meta_ref[...]
    f = meta[:, META_G0:META_G0 + 1] * buf_ref[slot, 0] + meta[:, META_G1:META_G1 + 1] * buf_ref[slot, 1]
    x_new = x_ref[...] + g2_ref[...] * f
    if final:
        ms = jnp.mean(x_new * x_new, axis=-1, keepdims=True)
        x_new = x_new * lax.rsqrt(ms + EPS) * fg_ref[...]
    o_ref[...] = x_new


def _combine(dest_flat, x, meta, g2, y_slots, final_g, final):
    per_seq = SEQ // COMB_TM
    return pl.pallas_call(
        functools.partial(_combine_kernel, final),
        grid_spec=pltpu.PrefetchScalarGridSpec(
            num_scalar_prefetch=1,
            grid=(TOKENS // COMB_TM,),
            in_specs=[
                pl.BlockSpec((COMB_TM, D_MODEL), lambda i, d: (i, 0)),
                pl.BlockSpec((COMB_TM, LANES), lambda i, d: (i, 0)),
                pl.BlockSpec((None, 1, D_MODEL), lambda i, d: (i // per_seq, 0, 0)),
                pl.BlockSpec((1, D_MODEL), lambda i, d: (0, 0)),
                pl.BlockSpec(memory_space=pl.ANY),
            ],
            out_specs=pl.BlockSpec((COMB_TM, D_MODEL), lambda i, d: (i, 0)),
            scratch_shapes=[pltpu.VMEM((2, TOP_K, COMB_TM, D_MODEL), F32),
                            pltpu.SemaphoreType.DMA((2,))],
        ),
        out_shape=jax.ShapeDtypeStruct((TOKENS, D_MODEL), F32),
        compiler_params=_params(1),
        name="moe_combine",
    )(dest_flat, x, meta, g2, final_g.reshape(1, D_MODEL), y_slots)


def _slot_plan(meta, counts):
    top_e = meta[:, META_E0:META_E1 + 1].astype(jnp.int32)
    rank = meta[:, META_R0:META_R1 + 1].astype(jnp.int32)
    counts = counts[0, :N_EXPERTS].astype(jnp.int32)
    n_blk = (counts + MOE_ROWS - 1) // MOE_ROWS
    first_blk = jnp.cumsum(n_blk) - n_blk
    dest = (first_blk[top_e] * MOE_ROWS + rank).reshape(TOKENS * TOP_K)
    pair_tok = jnp.arange(TOKENS * TOP_K, dtype=jnp.int32) // TOP_K
    slot_tok = jnp.zeros((N_SLOTS,), jnp.int32).at[dest].set(pair_tok)
    return dest, slot_tok, first_blk.astype(jnp.int32), n_blk.astype(jnp.int32)


def kernel(x, c, ada_w, ada_b, norm_g, conv_w_in, conv_k, conv_w_out, gla_w_in, gla_w_gk, gla_b_gk,
           gla_norm_g, gla_w_out, ffn_w13, ffn_w2, moe_router, moe_w13, moe_w2, final_g):
    assert x.shape == (BATCH, SEQ, D_MODEL) and x.dtype == F32
    mod = _ada_all(c, ada_w, ada_b)[:, :BATCH]
    xt = x.reshape(TOKENS, D_MODEL)
    gla_w_in_t = jnp.swapaxes(gla_w_in, 1, 2)
    for i in range(DEPTH):
        j = i // 2
        sh1, sc1, g1, sh2, sc2, g2 = (
            mod[i, :, n * D_MODEL:(n + 1) * D_MODEL].reshape(BATCH, 1, D_MODEL) for n in range(6))
        h = _norm_modulate(xt, norm_g[i, 0], sh1, sc1)
        if i % 2 == 0:
            y = _conv_in(h, conv_w_in, conv_k, j)
            xt = _matmul_residual(y, conv_w_out, j, xt, g1)
            h = _norm_modulate(xt, norm_g[i, 1], sh2, sc2)
            t = _ffn_up(h, ffn_w13, j)
            xt = _matmul_residual(t, ffn_w2, j, xt, g2)
        else:
            proj = _matmul_nt(h, gla_w_in_t, j, GLA_MAIN, BF16)
            log_a = _gla_log_decay(h, gla_w_in_t, j, gla_w_gk[j], gla_b_gk[j])
            o = _gla_scan(proj, log_a, gla_norm_g[j])
            xt = _matmul_residual(o, gla_w_out, j, xt, g1)
            h32, meta, counts = _route(xt, norm_g[i, 1], sh2, sc2, moe_router[j])
            dest, slot_tok, first_blk, n_blk = _slot_plan(meta, counts)
            xs = _gather_slots(slot_tok, h32)
            t = _grouped_matmul("moe_up", _swiglu_tile, first_blk, n_blk, xs, moe_w13, j,
                                (0, D_FF), D_FF, BF16)
            y = _grouped_matmul("moe_down", _down_tile, first_blk, n_blk, t, moe_w2, j,
                                (0,), D_MODEL, F32)
            xt = _combine(dest, xt, meta, g2, y, final_g, final=(i == DEPTH - 1))
    return xt.reshape(BATCH, SEQ, D_MODEL)
```

```python
import functools

import jax
import jax.numpy as jnp
from jax import lax
from jax.experimental import pallas as pl
from jax.experimental.pallas import tpu as pltpu

D_MODEL = 2048
BATCH = 4
SEQ = 2048
TOKENS = BATCH * SEQ
DEPTH = 4
CHUNK = 64
EPS = 1e-6
CONV_W = 3
GLA_HEADS = 4
GLA_DK = D_MODEL // 2
GLA_DV = D_MODEL
GLA_DK_HEAD = GLA_DK // GLA_HEADS
GLA_DV_HEAD = GLA_DV // GLA_HEADS
GLA_GATE_RANK = 16
GLA_GATE_NORM = 16.0
GLA_MAIN = 2 * GLA_DK + 2 * GLA_DV
D_FF = 5632
N_EXPERTS = 8
TOP_K = 2

LANES = 128
SUBLANES = 8
MXU_DIM = 256
VMEM_LIMIT = 56 * 1024 * 1024

MOE_ROWS = 256
N_SLOT_BLOCKS = TOKENS * TOP_K // MOE_ROWS + N_EXPERTS
N_SLOTS = N_SLOT_BLOCKS * MOE_ROWS

F32 = jnp.float32
BF16 = jnp.bfloat16


def _params(n_axes):
    return pltpu.CompilerParams(
        dimension_semantics=("arbitrary",) * n_axes, vmem_limit_bytes=VMEM_LIMIT)


def _dot(a, b):
    return jnp.dot(a, b, preferred_element_type=F32)


def _silu(v):
    return v * jax.nn.sigmoid(v)


ADA_TN = 1024


def _ada_kernel(c_ref, w_ref, b_ref, o_ref):
    c_act = _silu(c_ref[...]).astype(BF16)
    o_ref[...] = _dot(c_act, w_ref[...].astype(BF16)) + b_ref[...]


def _ada_all(c, ada_w, ada_b):
    c_pad = jnp.pad(c, ((0, SUBLANES - BATCH), (0, 0)))
    n = 6 * D_MODEL
    return pl.pallas_call(
        _ada_kernel,
        grid=(DEPTH, n // ADA_TN),
        in_specs=[
            pl.BlockSpec((SUBLANES, D_MODEL), lambda l, j: (0, 0)),
            pl.BlockSpec((None, D_MODEL, ADA_TN), lambda l, j: (l, 0, j)),
            pl.BlockSpec((None, 1, ADA_TN), lambda l, j: (l, 0, j)),
        ],
        out_specs=pl.BlockSpec((None, SUBLANES, ADA_TN), lambda l, j: (l, 0, j)),
        out_shape=jax.ShapeDtypeStruct((DEPTH, SUBLANES, n), F32),
        compiler_params=_params(2),
        name="ada_mod",
    )(c_pad, ada_w, ada_b.reshape(DEPTH, 1, n))


NORM_TM = 512


def _norm_mod(x, g, sc, sh):
    ms = jnp.mean(x * x, axis=-1, keepdims=True)
    y = x * lax.rsqrt(ms + EPS) * g
    return y * (1.0 + sc) + sh


def _norm_mod_kernel(x_ref, g_ref, sh_ref, sc_ref, h_ref):
    h_ref[...] = _norm_mod(x_ref[...], g_ref[...], sc_ref[...], sh_ref[...]).astype(h_ref.dtype)


def _row_specs(tm):
    per_seq = SEQ // tm
    rows = pl.BlockSpec((tm, D_MODEL), lambda i: (i, 0))
    vec = pl.BlockSpec((1, D_MODEL), lambda i: (0, 0))
    seq_vec = pl.BlockSpec((None, 1, D_MODEL), lambda i: (i // per_seq, 0, 0))
    return rows, vec, seq_vec


def _norm_modulate(x, g, sh, sc):
    rows, vec, seq_vec = _row_specs(NORM_TM)
    return pl.pallas_call(
        _norm_mod_kernel,
        grid=(TOKENS // NORM_TM,),
        in_specs=[rows, vec, seq_vec, seq_vec],
        out_specs=rows,
        out_shape=jax.ShapeDtypeStruct((TOKENS, D_MODEL), BF16),
        compiler_params=_params(1),
        name="norm_mod",
    )(x, g.reshape(1, D_MODEL), sh, sc)


MM_TM = 1024
MM_TN = 512


def _cast_weights_once(w_refs, wb_refs):
    @pl.when(pl.program_id(1) == 0)
    def _():
        for w_ref, wb_ref in zip(w_refs, wb_refs):
            wb_ref[...] = w_ref[...].astype(BF16)


def _dot_nt(a, b_t):
    return lax.dot_general(a, b_t, (((1,), (1,)), ((), ())), preferred_element_type=F32)


def _mm_nt_kernel(a_ref, wt_ref, o_ref, wb_ref):
    _cast_weights_once((wt_ref,), (wb_ref,))
    o_ref[...] = _dot_nt(a_ref[...], wb_ref[...]).astype(o_ref.dtype)


def _matmul_nt(a, wt3, layer, n_out, out_dtype):
    k = a.shape[1]
    tn = _wide_col_tile(k)
    return pl.pallas_call(
        _mm_nt_kernel,
        grid=(n_out // tn, TOKENS // MM_TM),
        in_specs=[
            pl.BlockSpec((MM_TM, k), lambda j, i: (i, 0)),
            pl.BlockSpec((None, tn, k), lambda j, i: (layer, j, 0)),
        ],
        out_specs=pl.BlockSpec((MM_TM, tn), lambda j, i: (i, j)),
        out_shape=jax.ShapeDtypeStruct((TOKENS, n_out), out_dtype),
        scratch_shapes=[pltpu.VMEM((tn, k), BF16)],
        compiler_params=_params(2),
        name="matmul_nt",
    )(a, wt3)


RES_ROW_BYTES = 6 * 1024 * 1024
WIDE_TILE_BYTES = 12 * 1024 * 1024


def _wide_col_tile(k):
    return min(2 * MM_TN, WIDE_TILE_BYTES // (4 * k) // MXU_DIM * MXU_DIM)


def _mm_res_kernel(a_ref, w_ref, x_ref, g_ref, o_ref, wb_ref):
    _cast_weights_once((w_ref,), (wb_ref,))
    o_ref[...] = x_ref[...] + g_ref[...] * _dot(a_ref[...], wb_ref[...])


def _matmul_residual(a, w3, layer, x, gate):
    k = a.shape[1]
    tm = min(MM_TM, RES_ROW_BYTES // (2 * k) // MXU_DIM * MXU_DIM)
    tn = _wide_col_tile(k)
    per_seq = SEQ // tm
    return pl.pallas_call(
        _mm_res_kernel,
        grid=(D_MODEL // tn, TOKENS // tm),
        in_specs=[
            pl.BlockSpec((tm, k), lambda j, i: (i, 0)),
            pl.BlockSpec((None, k, tn), lambda j, i: (layer, 0, j)),
            pl.BlockSpec((tm, tn), lambda j, i: (i, j)),
            pl.BlockSpec((None, 1, tn), lambda j, i: (i // per_seq, 0, j)),
        ],
        out_specs=pl.BlockSpec((tm, tn), lambda j, i: (i, j)),
        out_shape=jax.ShapeDtypeStruct((TOKENS, D_MODEL), F32),
        scratch_shapes=[pltpu.VMEM((k, tn), BF16)],
        compiler_params=_params(2),
        name="matmul_residual",
    )(a, w3, x, gate)


CONV_TM = 1024


def _conv_in_kernel(a_ref, wgb_ref, wgc_ref, wu_ref, kc_ref, o_ref, wb_ref, v_ref):
    i = pl.program_id(1)
    _cast_weights_once((wgb_ref, wgc_ref, wu_ref), (wb_ref.at[0], wb_ref.at[1], wb_ref.at[2]))

    @pl.when(i % (SEQ // CONV_TM) == 0)
    def _():
        v_ref[0:SUBLANES, :] = jnp.zeros((SUBLANES, MM_TN), F32)

    a = a_ref[...]
    gc = _dot(a, wb_ref[1])
    u = _dot(a, wb_ref[2])
    v_ref[SUBLANES:SUBLANES + CONV_TM, :] = gc * u
    conv = kc_ref[2:3, :] * v_ref[SUBLANES:SUBLANES + CONV_TM, :]
    conv = conv + kc_ref[1:2, :] * v_ref[SUBLANES - 1:SUBLANES - 1 + CONV_TM, :]
    conv = conv + kc_ref[0:1, :] * v_ref[SUBLANES - 2:SUBLANES - 2 + CONV_TM, :]
    gb = _dot(a, wb_ref[0])
    o_ref[...] = (gb * conv).astype(o_ref.dtype)
    v_ref[0:SUBLANES, :] = v_ref[CONV_TM:CONV_TM + SUBLANES, :]


def _conv_in(h, conv_w_in, conv_k, layer):
    nb = D_MODEL // MM_TN
    w_spec = lambda off: pl.BlockSpec((None, D_MODEL, MM_TN), lambda j, i: (layer, 0, j + off))
    return pl.pallas_call(
        _conv_in_kernel,
        grid=(nb, TOKENS // CONV_TM),
        in_specs=[
            pl.BlockSpec((CONV_TM, D_MODEL), lambda j, i: (i, 0)),
            w_spec(0), w_spec(nb), w_spec(2 * nb),
            pl.BlockSpec((None, CONV_W, MM_TN), lambda j, i: (layer, 0, j)),
        ],
        out_specs=pl.BlockSpec((CONV_TM, MM_TN), lambda j, i: (i, j)),
        out_shape=jax.ShapeDtypeStruct((TOKENS, D_MODEL), BF16),
        scratch_shapes=[
            pltpu.VMEM((3, D_MODEL, MM_TN), BF16),
            pltpu.VMEM((CONV_TM + SUBLANES, MM_TN), F32),
        ],
        compiler_params=_params(2),
        name="conv_in",
    )(h, conv_w_in, conv_w_in, conv_w_in, conv_k)


def _ffn_up_kernel(a_ref, w1_ref, w3_ref, o_ref, wb_ref):
    _cast_weights_once((w1_ref, w3_ref), (wb_ref.at[0], wb_ref.at[1]))
    a = a_ref[...]
    p = _dot(a, wb_ref[0])
    q = _dot(a, wb_ref[1])
    o_ref[...] = (_silu(p) * q).astype(o_ref.dtype)


def _ffn_up(h, w13, layer):
    nb = D_FF // MM_TN
    w_spec = lambda off: pl.BlockSpec((None, D_MODEL, MM_TN), lambda j, i: (layer, 0, j + off))
    return pl.pallas_call(
        _ffn_up_kernel,
        grid=(nb, TOKENS // MM_TM),
        in_specs=[pl.BlockSpec((MM_TM, D_MODEL), lambda j, i: (i, 0)), w_spec(0), w_spec(nb)],
        out_specs=pl.BlockSpec((MM_TM, MM_TN), lambda j, i: (i, j)),
        out_shape=jax.ShapeDtypeStruct((TOKENS, D_FF), BF16),
        scratch_shapes=[pltpu.VMEM((2, D_MODEL, MM_TN), BF16)],
        compiler_params=_params(2),
        name="ffn_up",
    )(h, w13, w13)


LOGA_TM = 512


def _loga_kernel(h_ref, wl_ref, wgk_ref, bgk_ref, o_ref):
    low = _dot_nt(h_ref[...], wl_ref[...].astype(BF16))
    z =_dot(low.astype(BF16), wgk_ref[...].astype(BF16)) + bgk_ref[...]
    log_sig = jnp.minimum(z, 0.0) - jnp.log1p(jnp.exp(-jnp.abs(z)))
    o_ref[...] = log_sig * (1.0 / GLA_GATE_NORM)


def _gla_log_decay(h, w_in_t, layer, w_gk, b_gk):
    return pl.pallas_call(
        _loga_kernel,
        grid=(TOKENS // LOGA_TM,),
        in_specs=[
            pl.BlockSpec((LOGA_TM, D_MODEL), lambda i: (i, 0)),
            pl.BlockSpec((None, GLA_GATE_RANK, D_MODEL), lambda i: (layer, GLA_MAIN // GLA_GATE_RANK, 0)),
            pl.BlockSpec((GLA_GATE_RANK, GLA_DK), lambda i: (0, 0)),
            pl.BlockSpec((1, GLA_DK), lambda i: (0, 0)),
        ],
        out_specs=pl.BlockSpec((LOGA_TM, GLA_DK), lambda i: (i, 0)),
        out_shape=jax.ShapeDtypeStruct((TOKENS, GLA_DK), F32),
        compiler_params=_params(1),
        name="gla_log_decay",
    )(h, w_in_t, w_gk, b_gk.reshape(1, GLA_DK))


GLA_ROWS = 512


def _gla_kernel(q_ref, k_ref, v_ref, g_ref, la_ref, ng_ref, o_ref, st_ref):
    @pl.when(pl.program_id(1) == 0)
    def _():
        st_ref[...] = jnp.zeros(st_ref.shape, F32)

    row = lax.broadcasted_iota(jnp.int32, (CHUNK, CHUNK), 0)
    col = lax.broadcasted_iota(jnp.int32, (CHUNK, CHUNK), 1)
    tri = (col <= row).astype(BF16)

    def chunk(ci, carry):
        rows = pl.ds(pl.multiple_of(ci * CHUNK, CHUNK), CHUNK)
        la = la_ref[rows, :]
        la_hi = la.astype(BF16)
        la_lo = (la - la_hi.astype(F32)).astype(BF16)
        bcum = _dot(tri, la_hi) + _dot(tri, la_lo)
        btot = bcum[CHUNK - 1:CHUNK, :]
        k_dec = (k_ref[rows, :].astype(F32) * jnp.exp(btot - bcum)).astype(BF16)
        decay = jnp.exp(btot)
        q = (q_ref[rows, :].astype(F32) * (GLA_DK_HEAD ** -0.5)).astype(BF16)
        for h in range(GLA_HEADS):
            kcols = slice(h * GLA_DK_HEAD, (h + 1) * GLA_DK_HEAD)
            vcols = slice(h * GLA_DV_HEAD, (h + 1) * GLA_DV_HEAD)
            kv_t = lax.dot_general(v_ref[rows, vcols], k_dec[:, kcols],
                                   (((0,), (0,)), ((), ())), preferred_element_type=F32)
            st = st_ref[h] * decay[:, kcols] + kv_t
            st_ref[h] = st
            o = lax.dot_general(q[:, kcols], st.astype(BF16),
                                (((1,), (1,)), ((), ())), preferred_element_type=F32)
            o = o * lax.rsqrt(jnp.mean(o * o, axis=-1, keepdims=True) + EPS) * ng_ref[...]
            o = o * _silu(g_ref[rows, vcols].astype(F32))
            o_ref[rows, vcols] = o.astype(o_ref.dtype)
        return carry

    lax.fori_loop(0, GLA_ROWS // CHUNK, chunk, 0)


def _gla_scan(proj, log_a, norm_g):
    per_seq = SEQ // GLA_ROWS
    rows = lambda width, blk: pl.BlockSpec((GLA_ROWS, width), lambda b, s: (b * per_seq + s, blk))
    return pl.pallas_call(
        _gla_kernel,
        grid=(BATCH, per_seq),
        in_specs=[
            rows(GLA_DK, 0), rows(GLA_DK, 1), rows(GLA_DV, 1), rows(GLA_DV, 2),
            rows(GLA_DK, 0),
            pl.BlockSpec((1, GLA_DV_HEAD), lambda b, s: (0, 0)),
        ],
        out_specs=rows(GLA_DV, 0),
        out_shape=jax.ShapeDtypeStruct((TOKENS, GLA_DV), BF16),
        scratch_shapes=[pltpu.VMEM((GLA_HEADS, GLA_DV_HEAD, GLA_DK_HEAD), F32)],
        compiler_params=_params(2),
        name="gla_scan",
    )(proj, proj, proj, proj, log_a, norm_g.reshape(1, GLA_DV_HEAD))


ROUTE_TM = 512
META_E0, META_E1, META_G0, META_G1, META_R0, META_R1 = range(6)


def _route_kernel(x_ref, g_ref, sh_ref, sc_ref, r_ref, h_ref, meta_ref, cnt_ref, run_ref):
    @pl.when(pl.program_id(0) == 0)
    def _():
        run_ref[...] = jnp.zeros(run_ref.shape, F32)

    h = _norm_mod(x_ref[...], g_ref[...], sc_ref[...], sh_ref[...])
    h_ref[...] = h
    logits =jnp.dot(h, r_ref[...], preferred_element_type=F32, precision=lax.Precision.HIGHEST)
    lane = lax.broadcasted_iota(jnp.int32, logits.shape, 1)
    logits = jnp.where(lane < N_EXPERTS, logits, -jnp.inf)
    m0 = jnp.max(logits, axis=1, keepdims=True)
    e0 = jnp.min(jnp.where(logits == m0, lane, LANES), axis=1, keepdims=True)
    rest = jnp.where(lane == e0, -jnp.inf, logits)
    m1 = jnp.max(rest, axis=1, keepdims=True)
    e1 = jnp.min(jnp.where(rest == m1, lane, LANES), axis=1, keepdims=True)
    p = jnp.exp(m1 - m0)
    gate0 = 1.0 / (1.0 + p)
    gate1 = p / (1.0 + p)

    hot0 = (lane == e0).astype(F32)
    hot1 = (lane == e1).astype(F32)
    both = hot0 + hot1
    row = lax.broadcasted_iota(jnp.int32, (ROUTE_TM, ROUTE_TM), 0)
    col = lax.broadcasted_iota(jnp.int32, (ROUTE_TM, ROUTE_TM), 1)
    before = _dot((col < row).astype(BF16), both.astype(BF16)) + run_ref[0:1, :]
    rank0 = jnp.sum(hot0 * before, axis=1, keepdims=True)
    rank1 = jnp.sum(hot1 * before, axis=1, keepdims=True)
    run_ref[...] = run_ref[...] + jnp.sum(both, axis=0, keepdims=True)
    cnt_ref[...] = run_ref[...]

    meta = jnp.zeros(logits.shape, F32)
    for lane_id, val in ((META_E0, e0.astype(F32)), (META_E1, e1.astype(F32)), (META_G0, gate0),
                         (META_G1, gate1), (META_R0, rank0), (META_R1, rank1)):
        meta = jnp.where(lane == lane_id, val, meta)
    meta_ref[...] = meta


def _route(x, g, sh, sc, router):
    rows, vec, seq_vec = _row_specs(ROUTE_TM)
    router = jnp.pad(router, ((0, 0), (0, LANES - N_EXPERTS)))
    return pl.pallas_call(
        _route_kernel,
        grid=(TOKENS // ROUTE_TM,),
        in_specs=[rows, vec, seq_vec, seq_vec, pl.BlockSpec((D_MODEL, LANES), lambda i: (0, 0))],
        out_specs=[
            rows,
            pl.BlockSpec((ROUTE_TM, LANES), lambda i: (i, 0)),
            pl.BlockSpec((SUBLANES, LANES), lambda i: (0, 0)),
        ],
        out_shape=[
            jax.ShapeDtypeStruct((TOKENS, D_MODEL), F32),
            jax.ShapeDtypeStruct((TOKENS, LANES), F32),
            jax.ShapeDtypeStruct((SUBLANES, LANES), F32),
        ],
        scratch_shapes=[pltpu.VMEM((SUBLANES, LANES), F32)],
        compiler_params=_params(1),
        name="moe_route",
    )(x, g.reshape(1, D_MODEL), sh, sc, router)


def _row_copy(src_hbm, src_row, dst_groups, group, sub, sem):
    return pltpu.make_async_copy(
        src_hbm.at[pl.ds(src_row, 1), :], dst_groups.at[group, pl.ds(sub, 1), :], sem)


def _for_rows(n_rows, fn):
    def body(group, carry):
        for sub in range(SUBLANES):
            fn(group, sub)
        return carry

    lax.fori_loop(0, n_rows // SUBLANES, body, 0)


def _gather_kernel(tok_ref, h_hbm, o_ref, buf_ref, sem):
    b = pl.program_id(0)

    def issue(blk):
        slot = blk & 1
        _for_rows(MOE_ROWS, lambda group, sub: _row_copy(
            h_hbm, tok_ref[blk * MOE_ROWS + group * SUBLANES + sub], buf_ref.at[slot], group, sub,
            sem.at[slot]).start())

    @pl.when(b == 0)
    def _():
        issue(b)

    @pl.when(b + 1 < pl.num_programs(0))
    def _():
        issue(b + 1)

    slot = b & 1
    _for_rows(MOE_ROWS, lambda group, sub: _row_copy(
        h_hbm, 0, buf_ref.at[slot], group, sub, sem.at[slot]).wait())
    o_ref[...] = buf_ref[slot].reshape(MOE_ROWS, D_MODEL).astype(o_ref.dtype)


def _gather_slots(slot_tok, h):
    return pl.pallas_call(
        _gather_kernel,
        grid_spec=pltpu.PrefetchScalarGridSpec(
            num_scalar_prefetch=1,
            grid=(N_SLOT_BLOCKS,),
            in_specs=[pl.BlockSpec(memory_space=pl.ANY)],
            out_specs=pl.BlockSpec((MOE_ROWS, D_MODEL), lambda b, tok: (b, 0)),
            scratch_shapes=[pltpu.VMEM((2, MOE_ROWS // SUBLANES, SUBLANES, D_MODEL), F32),
                            pltpu.SemaphoreType.DMA((2,))],
        ),
        out_shape=jax.ShapeDtypeStruct((N_SLOTS, D_MODEL), BF16),
        compiler_params=_params(1),
        name="moe_gather",
    )(slot_tok, h)


BLOCK_DMA_PRIORITY = 1
CHUNK_BLOCKS = 2


def _grouped_kernel(n_w, compute, first_ref, cnt_ref, a_hbm, *refs):
    w_refs, o_hbm = refs[:n_w], refs[n_w]
    wb_ref, a_buf, o_buf, sem_in, sem_out, pend_ref = refs[n_w + 1:]
    f, e = pl.program_id(0), pl.program_id(1)
    n, first = cnt_ref[e], first_ref[e]
    col = pl.multiple_of(f * MM_TN, MM_TN)
    n_full = n // CHUNK_BLOCKS
    has_tail = n % CHUNK_BLOCKS == 1
    tail_blk = first + CHUNK_BLOCKS * n_full
    tail_slot = n_full & 1

    def copy_in(blk, n_blocks, slot):
        rows = n_blocks * MOE_ROWS
        return pltpu.make_async_copy(
            a_hbm.at[pl.ds(blk * MOE_ROWS, rows), :], a_buf.at[slot, pl.ds(0, rows), :], sem_in.at[slot])

    def copy_out(blk, n_blocks, slot):
        rows = n_blocks * MOE_ROWS
        return pltpu.make_async_copy(
            o_buf.at[slot, pl.ds(0, rows), :],
            o_hbm.at[pl.ds(blk * MOE_ROWS, rows), pl.ds(col, MM_TN)], sem_out.at[slot])

    def start_first_chunk(expert):
        @pl.when(cnt_ref[expert] >= CHUNK_BLOCKS)
        def _():
            copy_in(first_ref[expert], CHUNK_BLOCKS, 0).start(priority=BLOCK_DMA_PRIORITY)

        @pl.when(cnt_ref[expert] == 1)
        def _():
            copy_in(first_ref[expert], 1, 0).start(priority=BLOCK_DMA_PRIORITY)

    def wait_out(slot):
        for n_blocks in (CHUNK_BLOCKS, 1):
            @pl.when(pend_ref[slot] == n_blocks)
            def _():
                copy_out(0, n_blocks, slot).wait()

        pend_ref[slot] = 0

    def start_out(blk, n_blocks, slot):
        copy_out(blk, n_blocks, slot).start(priority=BLOCK_DMA_PRIORITY)
        pend_ref[slot] = n_blocks

    for k in range(n_w):
        wb_ref[k] = w_refs[k][...].astype(BF16)

    @pl.when(jnp.logical_and(f == 0, e == 0))
    def _():
        pend_ref[0] = 0
        pend_ref[1] = 0
        start_first_chunk(e)

    def full_chunk(c, carry):
        slot = c & 1
        blk = first + CHUNK_BLOCKS * c
        copy_in(blk, CHUNK_BLOCKS, slot).wait()

        @pl.when(c + 1 < n_full)
        def _():
            copy_in(blk + CHUNK_BLOCKS, CHUNK_BLOCKS, 1 - slot).start(priority=BLOCK_DMA_PRIORITY)

        @pl.when(jnp.logical_and(c + 1 == n_full, has_tail))
        def _():
            copy_in(blk + CHUNK_BLOCKS, 1, 1 - slot).start(priority=BLOCK_DMA_PRIORITY)

        wait_out(slot)
        o_buf[slot] = compute(a_buf[slot], wb_ref).astype(o_buf.dtype)
        start_out(blk, CHUNK_BLOCKS, slot)
        return carry

    lax.fori_loop(0, n_full, full_chunk, 0)

    @pl.when(has_tail)
    def _():
        copy_in(tail_blk, 1, tail_slot).wait()
        wait_out(tail_slot)
        o_buf[tail_slot, 0:MOE_ROWS, :] = compute(
            a_buf[tail_slot, 0:MOE_ROWS, :], wb_ref).astype(o_buf.dtype)
        start_out(tail_blk, 1, tail_slot)

    last_e = e == N_EXPERTS - 1
    is_last_step = jnp.logical_and(last_e, f == pl.num_programs(0) - 1)

    @pl.when(jnp.logical_not(is_last_step))
    def _():
        start_first_chunk(jnp.where(last_e, 0, e + 1))

    @pl.when(last_e)
    def _():
        wait_out(0)
        o_buf[0, 0:MOE_ROWS, :] = jnp.zeros((MOE_ROWS, MM_TN), o_buf.dtype)

        def zero_block(blk, carry):
            copy_out(blk, 1, 0).start()
            copy_out(blk, 1, 0).wait()
            return carry

        lax.fori_loop(first + n, N_SLOT_BLOCKS, zero_block, 0)

    @pl.when(is_last_step)
    def _():
        wait_out(0)
        wait_out(1)


def _grouped_matmul(name, compute, first_blk, n_blk, a, w4, layer, col_offsets, n_out, out_dtype):
    k = a.shape[1]
    w_spec = lambda off: pl.BlockSpec(
        (None, None, k, MM_TN), lambda f, e, first, cnt: (layer, e, 0, f + off // MM_TN))
    n_w = len(col_offsets)
    return pl.pallas_call(
        functools.partial(_grouped_kernel, n_w, compute),
        grid_spec=pltpu.PrefetchScalarGridSpec(
            num_scalar_prefetch=2,
            grid=(n_out // MM_TN, N_EXPERTS),
            in_specs=[pl.BlockSpec(memory_space=pl.ANY)] + [w_spec(off) for off in col_offsets],
            out_specs=pl.BlockSpec(memory_space=pl.ANY),
            scratch_shapes=[
                pltpu.VMEM((n_w, k, MM_TN), BF16),
                pltpu.VMEM((2, CHUNK_BLOCKS * MOE_ROWS, k), BF16),
                pltpu.VMEM((2, CHUNK_BLOCKS * MOE_ROWS, MM_TN), out_dtype),
                pltpu.SemaphoreType.DMA((2,)),
                pltpu.SemaphoreType.DMA((2,)),
                pltpu.SMEM((2,), jnp.int32),
            ],
        ),
        out_shape=jax.ShapeDtypeStruct((N_SLOTS, n_out), out_dtype),
        compiler_params=_params(2),
        name=name,
    )(first_blk, n_blk, a, *([w4] * n_w))


def _swiglu_tile(a, wb_ref):
    return _silu(_dot(a, wb_ref[0])) * _dot(a, wb_ref[1])


def _down_tile(a, wb_ref):
    return _dot(a, wb_ref[0])


COMB_TM = 256


def _combine_kernel(final, dest_ref, x_ref, meta_ref, g2_ref, fg_ref, y_hbm, o_ref, buf_ref, sem):
    i = pl.program_id(0)

    def issue(tile):
        slot = tile & 1
        for k in range(TOP_K):
            _for_rows(COMB_TM, lambda group, sub: _row_copy(
                y_hbm, dest_ref[TOP_K * (tile * COMB_TM + group * SUBLANES + sub) + k],
                buf_ref.at[slot, k], group, sub, sem.at[slot]).start(priority=BLOCK_DMA_PRIORITY))

    @pl.when(i == 0)
    def _():
        issue(i)

    @pl.when(i + 1 < pl.num_programs(0))
    def _():
        issue(i + 1)

    slot = i & 1
    for k in range(TOP_K):
        _for_rows(COMB_TM, lambda group, sub: _row_copy(
            y_hbm, 0, buf_ref.at[slot, k], group, sub, sem.at[slot]).wait())
    meta = meta_ref[...]
    y0 = buf_ref[slot, 0].reshape(COMB_TM, D_MODEL)
    y1 = buf_ref[slot, 1].reshape(COMB_TM, D_MODEL)
    f = meta[:, META_G0:META_G0 + 1] * y0 + meta[:, META_G1:META_G1 + 1] * y1
    x_new = x_ref[...] + g2_ref[...] * f
    if final:
        ms = jnp.mean(x_new * x_new, axis=-1, keepdims=True)
        x_new = x_new * lax.rsqrt(ms + EPS) * fg_ref[...]
    o_ref[...] = x_new


def _combine(dest_flat, x, meta, g2, y_slots, final_g, final):
    per_seq = SEQ // COMB_TM
    return pl.pallas_call(
        functools.partial(_combine_kernel, final),
        grid_spec=pltpu.PrefetchScalarGridSpec(
            num_scalar_prefetch=1,
            grid=(TOKENS // COMB_TM,),
            in_specs=[
                pl.BlockSpec((COMB_TM, D_MODEL), lambda i, d: (i, 0)),
                pl.BlockSpec((COMB_TM, LANES), lambda i, d: (i, 0)),
                pl.BlockSpec((None, 1, D_MODEL), lambda i, d: (i // per_seq, 0, 0)),
                pl.BlockSpec((1, D_MODEL), lambda i, d: (0, 0)),
                pl.BlockSpec(memory_space=pl.ANY),
            ],
            out_specs=pl.BlockSpec((COMB_TM, D_MODEL), lambda i, d: (i, 0)),
            scratch_shapes=[pltpu.VMEM((2, TOP_K, COMB_TM // SUBLANES, SUBLANES, D_MODEL), F32),
                            pltpu.SemaphoreType.DMA((2,))],
        ),
        out_shape=jax.ShapeDtypeStruct((TOKENS, D_MODEL), F32),
        compiler_params=_params(1),
        name="moe_combine",
    )(dest_flat, x, meta, g2, final_g.reshape(1, D_MODEL), y_slots)


def _slot_plan(meta, counts):
    top_e = meta[:, META_E0:META_E1 + 1].astype(jnp.int32)
    rank = meta[:, META_R0:META_R1 + 1].astype(jnp.int32)
    counts = counts[0, :N_EXPERTS].astype(jnp.int32)
    n_blk = (counts + MOE_ROWS - 1) // MOE_ROWS
    first_blk = jnp.cumsum(n_blk) - n_blk
    dest = (first_blk[top_e] * MOE_ROWS + rank).reshape(TOKENS * TOP_K)
    pair_tok = jnp.arange(TOKENS * TOP_K, dtype=jnp.int32) // TOP_K
    slot_tok = jnp.zeros((N_SLOTS,), jnp.int32).at[dest].set(pair_tok)
    return dest, slot_tok, first_blk.astype(jnp.int32), n_blk.astype(jnp.int32)


def kernel(x, c, ada_w, ada_b, norm_g, conv_w_in, conv_k, conv_w_out, gla_w_in, gla_w_gk, gla_b_gk,
           gla_norm_g, gla_w_out, ffn_w13, ffn_w2, moe_router, moe_w13, moe_w2, final_g):
    assert x.shape == (BATCH, SEQ, D_MODEL) and x.dtype == F32
    mod = _ada_all(c, ada_w, ada_b)[:, :BATCH]
    xt = x.reshape(TOKENS, D_MODEL)
    gla_w_in_t = jnp.swapaxes(gla_w_in, 1, 2)
    for i in range(DEPTH):
        j = i // 2
        sh1, sc1, g1, sh2, sc2, g2 = (
            mod[i, :, n * D_MODEL:(n + 1) * D_MODEL].reshape(BATCH, 1, D_MODEL) for n in range(6))
        h = _norm_modulate(xt, norm_g[i, 0], sh1, sc1)
        if i % 2 == 0:
            y = _conv_in(h, conv_w_in, conv_k, j)
            xt = _matmul_residual(y, conv_w_out, j, xt, g1)
            h = _norm_modulate(xt, norm_g[i, 1], sh2, sc2)
            t = _ffn_up(h, ffn_w13, j)
            xt = _matmul_residual(t, ffn_w2, j, xt, g2)
        else:
            proj = _matmul_nt(h, gla_w_in_t, j, GLA_MAIN, BF16)
            log_a = _gla_log_decay(h, gla_w_in_t, j, gla_w_gk[j], gla_b_gk[j])
            o = _gla_scan(proj, log_a, gla_norm_g[j])
            xt = _matmul_residual(o, gla_w_out, j, xt, g1)
            h32, meta, counts = _route(xt, norm_g[i, 1], sh2, sc2, moe_router[j])
            dest, slot_tok, first_blk, n_blk = _slot_plan(meta, counts)
            xs = _gather_slots(slot_tok, h32)
            t = _grouped_matmul("moe_up", _swiglu_tile, first_blk, n_blk, xs, moe_w13, j,
                                (0, D_FF), D_FF, BF16)
            y = _grouped_matmul("moe_down", _down_tile, first_blk, n_blk, t, moe_w2, j,
                                (0,), D_MODEL, F32)
            xt = _combine(dest, xt, meta, g2, y, final_g, final=(i == DEPTH - 1))
    return xt.reshape(BATCH, SEQ, D_MODEL)
```

```python
import functools

import jax
import jax.numpy as jnp
from jax import lax
from jax.experimental import pallas as pl
from jax.experimental.pallas import tpu as pltpu

D_MODEL = 2048
BATCH = 4
SEQ = 2048
TOKENS = BATCH * SEQ
DEPTH = 4
CHUNK = 64
EPS = 1e-6
CONV_W = 3
GLA_HEADS = 4
GLA_DK = D_MODEL // 2
GLA_DV = D_MODEL
GLA_DK_HEAD = GLA_DK // GLA_HEADS
GLA_DV_HEAD = GLA_DV // GLA_HEADS
GLA_GATE_RANK = 16
GLA_GATE_NORM = 16.0
GLA_MAIN = 2 * GLA_DK + 2 * GLA_DV
D_FF = 5632
N_EXPERTS = 8
TOP_K = 2

LANES = 128
SUBLANES = 8
MXU_DIM = 256
VMEM_LIMIT = 56 * 1024 * 1024

MOE_ROWS = 256
N_SLOT_BLOCKS = TOKENS * TOP_K // MOE_ROWS + N_EXPERTS
N_SLOTS = N_SLOT_BLOCKS * MOE_ROWS

F32 = jnp.float32
BF16 = jnp.bfloat16


def _params(n_axes):
    return pltpu.CompilerParams(
        dimension_semantics=("arbitrary",) * n_axes, vmem_limit_bytes=VMEM_LIMIT)


def _dot(a, b):
    return jnp.dot(a, b, preferred_element_type=F32)


def _silu(v):
    return v * jax.nn.sigmoid(v)


ADA_TN = 1024


def _ada_kernel(c_ref, w_ref, b_ref, o_ref):
    c_act = _silu(c_ref[...]).astype(BF16)
    o_ref[...] = _dot(c_act, w_ref[...].astype(BF16)) + b_ref[...]


def _ada_all(c, ada_w, ada_b):
    c_pad = jnp.pad(c, ((0, SUBLANES - BATCH), (0, 0)))
    n = 6 * D_MODEL
    return pl.pallas_call(
        _ada_kernel,
        grid=(DEPTH, n // ADA_TN),
        in_specs=[
            pl.BlockSpec((SUBLANES, D_MODEL), lambda l, j: (0, 0)),
            pl.BlockSpec((None, D_MODEL, ADA_TN), lambda l, j: (l, 0, j)),
            pl.BlockSpec((None, 1, ADA_TN), lambda l, j: (l, 0, j)),
        ],
        out_specs=pl.BlockSpec((None, SUBLANES, ADA_TN), lambda l, j: (l, 0, j)),
        out_shape=jax.ShapeDtypeStruct((DEPTH, SUBLANES, n), F32),
        compiler_params=_params(2),
        name="ada_mod",
    )(c_pad, ada_w, ada_b.reshape(DEPTH, 1, n))


NORM_TM = 512


def _norm_mod(x, g, sc, sh):
    ms = jnp.mean(x * x, axis=-1, keepdims=True)
    y = x * lax.rsqrt(ms + EPS) * g
    return y * (1.0 + sc) + sh


def _norm_mod_kernel(x_ref, g_ref, sh_ref, sc_ref, h_ref):
    h_ref[...] = _norm_mod(x_ref[...], g_ref[...], sc_ref[...], sh_ref[...]).astype(h_ref.dtype)


def _row_specs(tm):
    per_seq = SEQ // tm
    rows = pl.BlockSpec((tm, D_MODEL), lambda i: (i, 0))
    vec = pl.BlockSpec((1, D_MODEL), lambda i: (0, 0))
    seq_vec = pl.BlockSpec((None, 1, D_MODEL), lambda i: (i // per_seq, 0, 0))
    return rows, vec, seq_vec


def _norm_modulate(x, g, sh, sc):
    rows, vec, seq_vec = _row_specs(NORM_TM)
    return pl.pallas_call(
        _norm_mod_kernel,
        grid=(TOKENS // NORM_TM,),
        in_specs=[rows, vec, seq_vec, seq_vec],
        out_specs=rows,
        out_shape=jax.ShapeDtypeStruct((TOKENS, D_MODEL), BF16),
        compiler_params=_params(1),
        name="norm_mod",
    )(x, g.reshape(1, D_MODEL), sh, sc)


MM_TM = 1024
MM_TN = 512


def _cast_weights_once(w_refs, wb_refs):
    @pl.when(pl.program_id(1) == 0)
    def _():
        for w_ref, wb_ref in zip(w_refs, wb_refs):
            wb_ref[...] = w_ref[...].astype(BF16)


def _dot_nt(a, b_t):
    return lax.dot_general(a, b_t, (((1,), (1,)), ((), ())), preferred_element_type=F32)


def _mm_nt_kernel(a_ref, wt_ref, o_ref, wb_ref):
    _cast_weights_once((wt_ref,), (wb_ref,))
    o_ref[...] = _dot_nt(a_ref[...], wb_ref[...]).astype(o_ref.dtype)


def _matmul_nt(a, wt3, layer, n_out, out_dtype):
    k = a.shape[1]
    tn = _wide_col_tile(k)
    return pl.pallas_call(
        _mm_nt_kernel,
        grid=(n_out // tn, TOKENS // MM_TM),
        in_specs=[
            pl.BlockSpec((MM_TM, k), lambda j, i: (i, 0)),
            pl.BlockSpec((None, tn, k), lambda j, i: (layer, j, 0)),
        ],
        out_specs=pl.BlockSpec((MM_TM, tn), lambda j, i: (i, j)),
        out_shape=jax.ShapeDtypeStruct((TOKENS, n_out), out_dtype),
        scratch_shapes=[pltpu.VMEM((tn, k), BF16)],
        compiler_params=_params(2),
        name="matmul_nt",
    )(a, wt3)


RES_ROW_BYTES = 6 * 1024 * 1024
WIDE_TILE_BYTES = 12 * 1024 * 1024


def _wide_col_tile(k):
    return min(2 * MM_TN, WIDE_TILE_BYTES // (4 * k) // MXU_DIM * MXU_DIM)


def _mm_res_kernel(a_ref, w_ref, x_ref, g_ref, o_ref, wb_ref):
    _cast_weights_once((w_ref,), (wb_ref,))
    o_ref[...] = x_ref[...] + g_ref[...] * _dot(a_ref[...], wb_ref[...])


def _matmul_residual(a, w3, layer, x, gate):
    k = a.shape[1]
    tm = min(MM_TM, RES_ROW_BYTES // (2 * k) // MXU_DIM * MXU_DIM)
    tn = _wide_col_tile(k)
    per_seq = SEQ // tm
    return pl.pallas_call(
        _mm_res_kernel,
        grid=(D_MODEL // tn, TOKENS // tm),
        in_specs=[
            pl.BlockSpec((tm, k), lambda j, i: (i, 0)),
            pl.BlockSpec((None, k, tn), lambda j, i: (layer, 0, j)),
            pl.BlockSpec((tm, tn), lambda j, i: (i, j)),
            pl.BlockSpec((None, 1, tn), lambda j, i: (i // per_seq, 0, j)),
        ],
        out_specs=pl.BlockSpec((tm, tn), lambda j, i: (i, j)),
        out_shape=jax.ShapeDtypeStruct((TOKENS, D_MODEL), F32),
        scratch_shapes=[pltpu.VMEM((k, tn), BF16)],
        compiler_params=_params(2),
        name="matmul_residual",
    )(a, w3, x, gate)


CONV_TM = 1024


def _conv_in_kernel(a_ref, wgb_ref, wgc_ref, wu_ref, kc_ref, o_ref, wb_ref, v_ref):
    i = pl.program_id(1)
    _cast_weights_once((wgb_ref, wgc_ref, wu_ref), (wb_ref.at[0], wb_ref.at[1], wb_ref.at[2]))

    @pl.when(i % (SEQ // CONV_TM) == 0)
    def _():
        v_ref[0:SUBLANES, :] = jnp.zeros((SUBLANES, MM_TN), F32)

    a = a_ref[...]
    gc = _dot(a, wb_ref[1])
    u = _dot(a, wb_ref[2])
    v_ref[SUBLANES:SUBLANES + CONV_TM, :] = gc * u
    conv = kc_ref[2:3, :] * v_ref[SUBLANES:SUBLANES + CONV_TM, :]
    conv = conv + kc_ref[1:2, :] * v_ref[SUBLANES - 1:SUBLANES - 1 + CONV_TM, :]
    conv = conv + kc_ref[0:1, :] * v_ref[SUBLANES - 2:SUBLANES - 2 + CONV_TM, :]
    gb = _dot(a, wb_ref[0])
    o_ref[...] = (gb * conv).astype(o_ref.dtype)
    v_ref[0:SUBLANES, :] = v_ref[CONV_TM:CONV_TM + SUBLANES, :]


def _conv_in(h, conv_w_in, conv_k, layer):
    nb = D_MODEL // MM_TN
    w_spec = lambda off: pl.BlockSpec((None, D_MODEL, MM_TN), lambda j, i: (layer, 0, j + off))
    return pl.pallas_call(
        _conv_in_kernel,
        grid=(nb, TOKENS // CONV_TM),
        in_specs=[
            pl.BlockSpec((CONV_TM, D_MODEL), lambda j, i: (i, 0)),
            w_spec(0), w_spec(nb), w_spec(2 * nb),
            pl.BlockSpec((None, CONV_W, MM_TN), lambda j, i: (layer, 0, j)),
        ],
        out_specs=pl.BlockSpec((CONV_TM, MM_TN), lambda j, i: (i, j)),
        out_shape=jax.ShapeDtypeStruct((TOKENS, D_MODEL), BF16),
        scratch_shapes=[
            pltpu.VMEM((3, D_MODEL, MM_TN), BF16),
            pltpu.VMEM((CONV_TM + SUBLANES, MM_TN), F32),
        ],
        compiler_params=_params(2),
        name="conv_in",
    )(h, conv_w_in, conv_w_in, conv_w_in, conv_k)


def _ffn_up_kernel(a_ref, w1_ref, w3_ref, o_ref, wb_ref):
    _cast_weights_once((w1_ref, w3_ref), (wb_ref.at[0], wb_ref.at[1]))
    a = a_ref[...]
    p = _dot(a, wb_ref[0])
    q = _dot(a, wb_ref[1])
    o_ref[...] = (_silu(p) * q).astype(o_ref.dtype)


def _ffn_up(h, w13, layer):
    nb = D_FF // MM_TN
    w_spec = lambda off: pl.BlockSpec((None, D_MODEL, MM_TN), lambda j, i: (layer, 0, j + off))
    return pl.pallas_call(
        _ffn_up_kernel,
        grid=(nb, TOKENS // MM_TM),
        in_specs=[pl.BlockSpec((MM_TM, D_MODEL), lambda j, i: (i, 0)), w_spec(0), w_spec(nb)],
        out_specs=pl.BlockSpec((MM_TM, MM_TN), lambda j, i: (i, j)),
        out_shape=jax.ShapeDtypeStruct((TOKENS, D_FF), BF16),
        scratch_shapes=[pltpu.VMEM((2, D_MODEL, MM_TN), BF16)],
        compiler_params=_params(2),
        name="ffn_up",
    )(h, w13, w13)


LOGA_TM = 512


def _loga_kernel(h_ref, wl_ref, wgk_ref, bgk_ref, o_ref):
    low = _dot_nt(h_ref[...], wl_ref[...].astype(BF16))
    z =_dot(low.astype(BF16), wgk_ref[...].astype(BF16)) + bgk_ref[...]
    log_sig = jnp.minimum(z, 0.0) - jnp.log1p(jnp.exp(-jnp.abs(z)))
    o_ref[...] = log_sig * (1.0 / GLA_GATE_NORM)


def _gla_log_decay(h, w_in_t, layer, w_gk, b_gk):
    return pl.pallas_call(
        _loga_kernel,
        grid=(TOKENS // LOGA_TM,),
        in_specs=[
            pl.BlockSpec((LOGA_TM, D_MODEL), lambda i: (i, 0)),
            pl.BlockSpec((None, GLA_GATE_RANK, D_MODEL), lambda i: (layer, GLA_MAIN // GLA_GATE_RANK, 0)),
            pl.BlockSpec((GLA_GATE_RANK, GLA_DK), lambda i: (0, 0)),
            pl.BlockSpec((1, GLA_DK), lambda i: (0, 0)),
        ],
        out_specs=pl.BlockSpec((LOGA_TM, GLA_DK), lambda i: (i, 0)),
        out_shape=jax.ShapeDtypeStruct((TOKENS, GLA_DK), F32),
        compiler_params=_params(1),
        name="gla_log_decay",
    )(h, w_in_t, w_gk, b_gk.reshape(1, GLA_DK))


GLA_ROWS = 512


def _gla_kernel(q_ref, k_ref, v_ref, g_ref, la_ref, ng_ref, o_ref, st_ref):
    @pl.when(pl.program_id(1) == 0)
    def _():
        st_ref[...] = jnp.zeros(st_ref.shape, F32)

    row = lax.broadcasted_iota(jnp.int32, (CHUNK, CHUNK), 0)
    col = lax.broadcasted_iota(jnp.int32, (CHUNK, CHUNK), 1)
    tri = (col <= row).astype(BF16)

    def chunk(ci, carry):
        rows = pl.ds(pl.multiple_of(ci * CHUNK, CHUNK), CHUNK)
        la = la_ref[rows, :]
        la_hi = la.astype(BF16)
        la_lo = (la - la_hi.astype(F32)).astype(BF16)
        bcum = _dot(tri, la_hi) + _dot(tri, la_lo)
        btot = bcum[CHUNK - 1:CHUNK, :]
        k_dec = (k_ref[rows, :].astype(F32) * jnp.exp(btot - bcum)).astype(BF16)
        decay = jnp.exp(btot)
        q = (q_ref[rows, :].astype(F32) * (GLA_DK_HEAD ** -0.5)).astype(BF16)
        for h in range(GLA_HEADS):
            kcols = slice(h * GLA_DK_HEAD, (h + 1) * GLA_DK_HEAD)
            vcols = slice(h * GLA_DV_HEAD, (h + 1) * GLA_DV_HEAD)
            kv_t = lax.dot_general(v_ref[rows, vcols], k_dec[:, kcols],
                                   (((0,), (0,)), ((), ())), preferred_element_type=F32)
            st = st_ref[h] * decay[:, kcols] + kv_t
            st_ref[h] = st
            o = lax.dot_general(q[:, kcols], st.astype(BF16),
                                (((1,), (1,)), ((), ())), preferred_element_type=F32)
            o = o * lax.rsqrt(jnp.mean(o * o, axis=-1, keepdims=True) + EPS) * ng_ref[...]
            o = o * _silu(g_ref[rows, vcols].astype(F32))
            o_ref[rows, vcols] = o.astype(o_ref.dtype)
        return carry

    lax.fori_loop(0, GLA_ROWS // CHUNK, chunk, 0)


def _gla_scan(proj, log_a, norm_g):
    per_seq = SEQ // GLA_ROWS
    rows = lambda width, blk: pl.BlockSpec((GLA_ROWS, width), lambda b, s: (b * per_seq + s, blk))
    return pl.pallas_call(
        _gla_kernel,
        grid=(BATCH, per_seq),
        in_specs=[
            rows(GLA_DK, 0), rows(GLA_DK, 1), rows(GLA_DV, 1), rows(GLA_DV, 2),
            rows(GLA_DK, 0),
            pl.BlockSpec((1, GLA_DV_HEAD), lambda b, s: (0, 0)),
        ],
        out_specs=rows(GLA_DV, 0),
        out_shape=jax.ShapeDtypeStruct((TOKENS, GLA_DV), BF16),
        scratch_shapes=[pltpu.VMEM((GLA_HEADS, GLA_DV_HEAD, GLA_DK_HEAD), F32)],
        compiler_params=_params(2),
        name="gla_scan",
    )(proj, proj, proj, proj, log_a, norm_g.reshape(1, GLA_DV_HEAD))


ROUTE_TM = 512
META_E0, META_E1, META_G0, META_G1, META_R0, META_R1 = range(6)


def _route_kernel(x_ref, g_ref, sh_ref, sc_ref, r_ref, h_ref, meta_ref, cnt_ref, run_ref):
    @pl.when(pl.program_id(0) == 0)
    def _():
        run_ref[...] = jnp.zeros(run_ref.shape, F32)

    h = _norm_mod(x_ref[...], g_ref[...], sc_ref[...], sh_ref[...])
    h_ref[...] = h
    logits =jnp.dot(h, r_ref[...], preferred_element_type=F32, precision=lax.Precision.HIGHEST)
    lane = lax.broadcasted_iota(jnp.int32, logits.shape, 1)
    logits = jnp.where(lane < N_EXPERTS, logits, -jnp.inf)
    m0 = jnp.max(logits, axis=1, keepdims=True)
    e0 = jnp.min(jnp.where(logits == m0, lane, LANES), axis=1, keepdims=True)
    rest = jnp.where(lane == e0, -jnp.inf, logits)
    m1 = jnp.max(rest, axis=1, keepdims=True)
    e1 = jnp.min(jnp.where(rest == m1, lane, LANES), axis=1, keepdims=True)
    p = jnp.exp(m1 - m0)
    gate0 = 1.0 / (1.0 + p)
    gate1 = p / (1.0 + p)

    hot0 = (lane == e0).astype(F32)
    hot1 = (lane == e1).astype(F32)
    both = hot0 + hot1
    row = lax.broadcasted_iota(jnp.int32, (ROUTE_TM, ROUTE_TM), 0)
    col = lax.broadcasted_iota(jnp.int32, (ROUTE_TM, ROUTE_TM), 1)
    before = _dot((col < row).astype(BF16), both.astype(BF16)) + run_ref[0:1, :]
    rank0 = jnp.sum(hot0 * before, axis=1, keepdims=True)
    rank1 = jnp.sum(hot1 * before, axis=1, keepdims=True)
    run_ref[...] = run_ref[...] + jnp.sum(both, axis=0, keepdims=True)
    cnt_ref[...] = run_ref[...]

    meta = jnp.zeros(logits.shape, F32)
    for lane_id, val in ((META_E0, e0.astype(F32)), (META_E1, e1.astype(F32)), (META_G0, gate0),
                         (META_G1, gate1), (META_R0, rank0), (META_R1, rank1)):
        meta = jnp.where(lane == lane_id, val, meta)
    meta_ref[...] = meta


def _route(x, g, sh, sc, router):
    rows, vec, seq_vec = _row_specs(ROUTE_TM)
    router = jnp.pad(router, ((0, 0), (0, LANES - N_EXPERTS)))
    return pl.pallas_call(
        _route_kernel,
        grid=(TOKENS // ROUTE_TM,),
        in_specs=[rows, vec, seq_vec, seq_vec, pl.BlockSpec((D_MODEL, LANES), lambda i: (0, 0))],
        out_specs=[
            rows,
            pl.BlockSpec((ROUTE_TM, LANES), lambda i: (i, 0)),
            pl.BlockSpec((SUBLANES, LANES), lambda i: (0, 0)),
        ],
        out_shape=[
            jax.ShapeDtypeStruct((TOKENS, D_MODEL), F32),
            jax.ShapeDtypeStruct((TOKENS, LANES), F32),
            jax.ShapeDtypeStruct((SUBLANES, LANES), F32),
        ],
        scratch_shapes=[pltpu.VMEM((SUBLANES, LANES), F32)],
        compiler_params=_params(1),
        name="moe_route",
    )(x, g.reshape(1, D_MODEL), sh, sc, router)


def _row_copy(src_hbm, src_row, dst_groups, group, sub, sem):
    return pltpu.make_async_copy(
        src_hbm.at[pl.ds(src_row, 1), :], dst_groups.at[group, pl.ds(sub, 1), :], sem)


def _for_rows(n_rows, fn):
    def body(group, carry):
        for sub in range(SUBLANES):
            fn(group, sub)
        return carry

    lax.fori_loop(0, n_rows // SUBLANES, body, 0)


def _gather_kernel(tok_ref, h_hbm, o_ref, buf_ref, sem):
    b = pl.program_id(0)

    def issue(blk):
        slot = blk & 1
        _for_rows(MOE_ROWS, lambda group, sub: _row_copy(
            h_hbm, tok_ref[blk * MOE_ROWS + group * SUBLANES + sub], buf_ref.at[slot], group, sub,
            sem.at[slot]).start())

    @pl.when(b == 0)
    def _():
        issue(b)

    @pl.when(b + 1 < pl.num_programs(0))
    def _():
        issue(b + 1)

    slot = b & 1
    _for_rows(MOE_ROWS, lambda group, sub: _row_copy(
        h_hbm, 0, buf_ref.at[slot], group, sub, sem.at[slot]).wait())
    o_ref[...] = buf_ref[slot].reshape(MOE_ROWS, D_MODEL).astype(o_ref.dtype)


def _gather_slots(slot_tok, h):
    return pl.pallas_call(
        _gather_kernel,
        grid_spec=pltpu.PrefetchScalarGridSpec(
            num_scalar_prefetch=1,
            grid=(N_SLOT_BLOCKS,),
            in_specs=[pl.BlockSpec(memory_space=pl.ANY)],
            out_specs=pl.BlockSpec((MOE_ROWS, D_MODEL), lambda b, tok: (b, 0)),
            scratch_shapes=[pltpu.VMEM((2, MOE_ROWS // SUBLANES, SUBLANES, D_MODEL), F32),
                            pltpu.SemaphoreType.DMA((2,))],
        ),
        out_shape=jax.ShapeDtypeStruct((N_SLOTS, D_MODEL), BF16),
        compiler_params=_params(1),
        name="moe_gather",
    )(slot_tok, h)


BLOCK_DMA_PRIORITY = 1
WEIGHT_DMA_PRIORITY = 1
CHUNK_BLOCKS = 2


def _grouped_kernel(layer, col_offsets, compute, first_ref, cnt_ref, a_hbm, w_hbm, o_hbm,
                    wb_ref, w_buf, a_buf, o_buf, sem_w, sem_in, sem_out):
    f, e = pl.program_id(0), pl.program_id(1)
    n, first = cnt_ref[e], first_ref[e]
    col = pl.multiple_of(f * MM_TN, MM_TN)
    n_full = n // CHUNK_BLOCKS
    has_tail = n % CHUNK_BLOCKS == 1
    tail_blk = first + CHUNK_BLOCKS * n_full
    tail_slot = n_full & 1
    last_e = e == N_EXPERTS - 1
    is_first_step = jnp.logical_and(f == 0, e == 0)
    is_last_step = jnp.logical_and(last_e, f == pl.num_programs(0) - 1)
    e_next = jnp.where(last_e, 0, e + 1)
    f_next = jnp.where(last_e, f + 1, f)
    w_slot = (f * N_EXPERTS + e) & 1

    def weight_copies(tile, expert, slot):
        return [pltpu.make_async_copy(
            w_hbm.at[layer, expert, :, pl.ds(pl.multiple_of(tile * MM_TN + off, MM_TN), MM_TN)],
            w_buf.at[slot, k], sem_w.at[slot]) for k, off in enumerate(col_offsets)]

    def copy_in(blk, n_blocks, slot):
        rows = n_blocks * MOE_ROWS
        return pltpu.make_async_copy(
            a_hbm.at[pl.ds(blk * MOE_ROWS, rows), :], a_buf.at[slot, pl.ds(0, rows), :], sem_in.at[slot])

    def copy_out(blk, n_blocks, slot):
        rows = n_blocks * MOE_ROWS
        return pltpu.make_async_copy(
            o_buf.at[slot, pl.ds(0, rows), :],
            o_hbm.at[pl.ds(blk * MOE_ROWS, rows), pl.ds(col, MM_TN)], sem_out.at[slot])

    def start_first_chunk(expert):
        @pl.when(cnt_ref[expert] >= CHUNK_BLOCKS)
        def _():
            copy_in(first_ref[expert], CHUNK_BLOCKS, 0).start()

        @pl.when(cnt_ref[expert] == 1)
        def _():
            copy_in(first_ref[expert], 1, 0).start()

    @pl.when(is_first_step)
    def _():
        for cp in weight_copies(f, e, w_slot):
            cp.start()
        start_first_chunk(e)

    for cp in weight_copies(f, e, w_slot):
        cp.wait()
    for k in range(len(col_offsets)):
        wb_ref[k] = w_buf[w_slot, k].astype(BF16)

    @pl.when(jnp.logical_not(is_last_step))
    def _():
        for cp in weight_copies(f_next, e_next, 1 - w_slot):
            cp.start(priority=WEIGHT_DMA_PRIORITY)

    def full_chunk(c, carry):
        slot = c & 1
        blk = first + CHUNK_BLOCKS * c
        copy_in(blk, CHUNK_BLOCKS, slot).wait()

        @pl.when(c + 1 < n_full)
        def _():
            copy_in(blk + CHUNK_BLOCKS, CHUNK_BLOCKS, 1 - slot).start()

        @pl.when(jnp.logical_and(c + 1 == n_full, has_tail))
        def _():
            copy_in(blk + CHUNK_BLOCKS, 1, 1 - slot).start()

        @pl.when(c >= 2)
        def _():
            copy_out(blk - 2 * CHUNK_BLOCKS, CHUNK_BLOCKS, slot).wait()

        o_buf[slot] = compute(a_buf[slot], wb_ref).astype(o_buf.dtype)
        copy_out(blk, CHUNK_BLOCKS, slot).start()
        return carry

    lax.fori_loop(0, n_full, full_chunk, 0)

    @pl.when(has_tail)
    def _():
        copy_in(tail_blk, 1, tail_slot).wait()

        @pl.when(n_full >= 2)
        def _():
            copy_out(tail_blk - 2 * CHUNK_BLOCKS, CHUNK_BLOCKS, tail_slot).wait()

        o_buf[tail_slot, 0:MOE_ROWS, :] = compute(
            a_buf[tail_slot, 0:MOE_ROWS, :], wb_ref).astype(o_buf.dtype)
        copy_out(tail_blk, 1, tail_slot).start()

    @pl.when(jnp.logical_not(is_last_step))
    def _():
        start_first_chunk(e_next)

    @pl.when(jnp.logical_and(n_full >= 2, jnp.logical_not(has_tail)))
    def _():
        copy_out(tail_blk - 2 * CHUNK_BLOCKS, CHUNK_BLOCKS, tail_slot).wait()

    @pl.when(n_full >= 1)
    def _():
        copy_out(tail_blk - CHUNK_BLOCKS, CHUNK_BLOCKS, 1 - tail_slot).wait()

    @pl.when(has_tail)
    def _():
        copy_out(tail_blk, 1, tail_slot).wait()

    @pl.when(last_e)
    def _():
        o_buf[0, 0:MOE_ROWS, :] = jnp.zeros((MOE_ROWS, MM_TN), o_buf.dtype)

        def zero_block(blk, carry):
            copy_out(blk, 1, 0).start()
            copy_out(blk, 1, 0).wait()
            return carry

        lax.fori_loop(first + n, N_SLOT_BLOCKS, zero_block, 0)


def _grouped_matmul(name, compute, first_blk, n_blk, a, w4, layer, col_offsets, n_out, out_dtype):
    k = a.shape[1]
    n_w = len(col_offsets)
    return pl.pallas_call(
        functools.partial(_grouped_kernel, layer, col_offsets, compute),
        grid_spec=pltpu.PrefetchScalarGridSpec(
            num_scalar_prefetch=2,
            grid=(n_out // MM_TN, N_EXPERTS),
            in_specs=[pl.BlockSpec(memory_space=pl.ANY), pl.BlockSpec(memory_space=pl.ANY)],
            out_specs=pl.BlockSpec(memory_space=pl.ANY),
            scratch_shapes=[
                pltpu.VMEM((n_w, k, MM_TN), BF16),
                pltpu.VMEM((2, n_w, k, MM_TN), F32),
                pltpu.VMEM((2, CHUNK_BLOCKS * MOE_ROWS, k), BF16),
                pltpu.VMEM((2, CHUNK_BLOCKS * MOE_ROWS, MM_TN), out_dtype),
                pltpu.SemaphoreType.DMA((2,)),
                pltpu.SemaphoreType.DMA((2,)),
                pltpu.SemaphoreType.DMA((2,)),
            ],
        ),
        out_shape=jax.ShapeDtypeStruct((N_SLOTS, n_out), out_dtype),
        compiler_params=_params(2),
        name=name,
    )(first_blk, n_blk, a, w4)


def _swiglu_tile(a, wb_ref):
    return _silu(_dot(a, wb_ref[0])) * _dot(a, wb_ref[1])


def _down_tile(a, wb_ref):
    return _dot(a, wb_ref[0])


COMB_TM = 256


def _combine_kernel(final, dest_ref, x_ref, meta_ref, g2_ref, fg_ref, y_hbm, o_ref, buf_ref, sem):
    i = pl.program_id(0)

    def issue(tile):
        slot = tile & 1
        for k in range(TOP_K):
            _for_rows(COMB_TM, lambda group, sub: _row_copy(
                y_hbm, dest_ref[TOP_K * (tile * COMB_TM + group * SUBLANES + sub) + k],
                buf_ref.at[slot, k], group, sub, sem.at[slot]).start(priority=BLOCK_DMA_PRIORITY))

    @pl.when(i == 0)
    def _():
        issue(i)

    @pl.when(i + 1 < pl.num_programs(0))
    def _():
        issue(i + 1)

    slot = i & 1
    for k in range(TOP_K):
        _for_rows(COMB_TM, lambda group, sub: _row_copy(
            y_hbm, 0, buf_ref.at[slot, k], group, sub, sem.at[slot]).wait())
    meta = meta_ref[...]
    y0 = buf_ref[slot, 0].reshape(COMB_TM, D_MODEL)
    y1 = buf_ref[slot, 1].reshape(COMB_TM, D_MODEL)
    f = meta[:, META_G0:META_G0 + 1] * y0 + meta[:, META_G1:META_G1 + 1] * y1
    x_new = x_ref[...] + g2_ref[...] * f
    if final:
        ms = jnp.mean(x_new * x_new, axis=-1, keepdims=True)
        x_new = x_new * lax.rsqrt(ms + EPS) * fg_ref[...]
    o_ref[...] = x_new


def _combine(dest_flat, x, meta, g2, y_slots, final_g, final):
    per_seq = SEQ // COMB_TM
    return pl.pallas_call(
        functools.partial(_combine_kernel, final),
        grid_spec=pltpu.PrefetchScalarGridSpec(
            num_scalar_prefetch=1,
            grid=(TOKENS // COMB_TM,),
            in_specs=[
                pl.BlockSpec((COMB_TM, D_MODEL), lambda i, d: (i, 0)),
                pl.BlockSpec((COMB_TM, LANES), lambda i, d: (i, 0)),
                pl.BlockSpec((None, 1, D_MODEL), lambda i, d: (i // per_seq, 0, 0)),
                pl.BlockSpec((1, D_MODEL), lambda i, d: (0, 0)),
                pl.BlockSpec(memory_space=pl.ANY),
            ],
            out_specs=pl.BlockSpec((COMB_TM, D_MODEL), lambda i, d: (i, 0)),
            scratch_shapes=[pltpu.VMEM((2, TOP_K, COMB_TM // SUBLANES, SUBLANES, D_MODEL), F32),
                            pltpu.SemaphoreType.DMA((2,))],
        ),
        out_shape=jax.ShapeDtypeStruct((TOKENS, D_MODEL), F32),
        compiler_params=_params(1),
        name="moe_combine",
    )(dest_flat, x, meta, g2, final_g.reshape(1, D_MODEL), y_slots)


def _slot_plan(meta, counts):
    top_e = meta[:, META_E0:META_E1 + 1].astype(jnp.int32)
    rank = meta[:, META_R0:META_R1 + 1].astype(jnp.int32)
    counts = counts[0, :N_EXPERTS].astype(jnp.int32)
    n_blk = (counts + MOE_ROWS - 1) // MOE_ROWS
    first_blk = jnp.cumsum(n_blk) - n_blk
    dest = (first_blk[top_e] * MOE_ROWS + rank).reshape(TOKENS * TOP_K)
    pair_tok = jnp.arange(TOKENS * TOP_K, dtype=jnp.int32) // TOP_K
    slot_tok = jnp.zeros((N_SLOTS,), jnp.int32).at[dest].set(pair_tok)
    return dest, slot_tok, first_blk.astype(jnp.int32), n_blk.astype(jnp.int32)


def kernel(x, c, ada_w, ada_b, norm_g, conv_w_in, conv_k, conv_w_out, gla_w_in, gla_w_gk, gla_b_gk,
           gla_norm_g, gla_w_out, ffn_w13, ffn_w2, moe_router, moe_w13, moe_w2, final_g):
    assert x.shape == (BATCH, SEQ, D_MODEL) and x.dtype == F32
    mod = _ada_all(c, ada_w, ada_b)[:, :BATCH]
    xt = x.reshape(TOKENS, D_MODEL)
    gla_w_in_t = jnp.swapaxes(gla_w_in, 1, 2)
    for i in range(DEPTH):
        j = i // 2
        sh1, sc1, g1, sh2, sc2, g2 = (
            mod[i, :, n * D_MODEL:(n + 1) * D_MODEL].reshape(BATCH, 1, D_MODEL) for n in range(6))
        h = _norm_modulate(xt, norm_g[i, 0], sh1, sc1)
        if i % 2 == 0:
            y = _conv_in(h, conv_w_in, conv_k, j)
            xt = _matmul_residual(y, conv_w_out, j, xt, g1)
            h = _norm_modulate(xt, norm_g[i, 1], sh2, sc2)
            t = _ffn_up(h, ffn_w13, j)
            xt = _matmul_residual(t, ffn_w2, j, xt, g2)
        else:
            proj = _matmul_nt(h, gla_w_in_t, j, GLA_MAIN, BF16)
            log_a = _gla_log_decay(h, gla_w_in_t, j, gla_w_gk[j], gla_b_gk[j])
            o = _gla_scan(proj, log_a, gla_norm_g[j])
            xt = _matmul_residual(o, gla_w_out, j, xt, g1)
            h32, meta, counts = _route(xt, norm_g[i, 1], sh2, sc2, moe_router[j])
            dest, slot_tok, first_blk, n_blk = _slot_plan(meta, counts)
            xs = _gather_slots(slot_tok, h32)
            t = _grouped_matmul("moe_up", _swiglu_tile, first_blk, n_blk, xs, moe_w13, j,
                                (0, D_FF), D_FF, BF16)
            y = _grouped_matmul("moe_down", _down_tile, first_blk, n_blk, t, moe_w2, j,
                                (0,), D_MODEL, F32)
            xt = _combine(dest, xt, meta, g2, y, final_g, final=(i == DEPTH - 1))
    return xt.reshape(BATCH, SEQ, D_MODEL)
```

```python
import functools

import jax
import jax.numpy as jnp
from jax import lax
from jax.experimental import pallas as pl
from jax.experimental.pallas import tpu as pltpu

D_MODEL = 2048
BATCH = 4
SEQ = 2048
TOKENS = BATCH * SEQ
DEPTH = 4
CHUNK = 64
EPS = 1e-6
CONV_W = 3
GLA_HEADS = 4
GLA_DK = D_MODEL // 2
GLA_DV = D_MODEL
GLA_DK_HEAD = GLA_DK // GLA_HEADS
GLA_DV_HEAD = GLA_DV // GLA_HEADS
GLA_GATE_RANK = 16
GLA_GATE_NORM = 16.0
GLA_MAIN = 2 * GLA_DK + 2 * GLA_DV
D_FF = 5632
N_EXPERTS = 8
TOP_K = 2

LANES = 128
SUBLANES = 8
MXU_DIM = 256
VMEM_LIMIT = 56 * 1024 * 1024

MOE_ROWS = 256
N_SLOT_BLOCKS = TOKENS * TOP_K // MOE_ROWS + N_EXPERTS
N_SLOTS = N_SLOT_BLOCKS * MOE_ROWS

F32 = jnp.float32
BF16 = jnp.bfloat16


def _params(n_axes):
    return pltpu.CompilerParams(
        dimension_semantics=("arbitrary",) * n_axes, vmem_limit_bytes=VMEM_LIMIT)


def _dot(a, b):
    return jnp.dot(a, b, preferred_element_type=F32)


def _silu(v):
    return v * jax.nn.sigmoid(v)


ADA_TN = 1024


def _ada_kernel(c_ref, w_ref, b_ref, o_ref):
    c_act = _silu(c_ref[...]).astype(BF16)
    o_ref[...] = _dot(c_act, w_ref[...].astype(BF16)) + b_ref[...]


def _ada_all(c, ada_w, ada_b):
    c_pad = jnp.pad(c, ((0, SUBLANES - BATCH), (0, 0)))
    n = 6 * D_MODEL
    return pl.pallas_call(
        _ada_kernel,
        grid=(DEPTH, n // ADA_TN),
        in_specs=[
            pl.BlockSpec((SUBLANES, D_MODEL), lambda l, j: (0, 0)),
            pl.BlockSpec((None, D_MODEL, ADA_TN), lambda l, j: (l, 0, j)),
            pl.BlockSpec((None, 1, ADA_TN), lambda l, j: (l, 0, j)),
        ],
        out_specs=pl.BlockSpec((None, SUBLANES, ADA_TN), lambda l, j: (l, 0, j)),
        out_shape=jax.ShapeDtypeStruct((DEPTH, SUBLANES, n), F32),
        compiler_params=_params(2),
        name="ada_mod",
    )(c_pad, ada_w, ada_b.reshape(DEPTH, 1, n))


NORM_TM = 512


def _norm_mod(x, g, sc, sh):
    ms = jnp.mean(x * x, axis=-1, keepdims=True)
    y = x * lax.rsqrt(ms + EPS) * g
    return y * (1.0 + sc) + sh


def _norm_mod_kernel(x_ref, g_ref, sh_ref, sc_ref, h_ref):
    h_ref[...] = _norm_mod(x_ref[...], g_ref[...], sc_ref[...], sh_ref[...]).astype(h_ref.dtype)


def _row_specs(tm):
    per_seq = SEQ // tm
    rows = pl.BlockSpec((tm, D_MODEL), lambda i: (i, 0))
    vec = pl.BlockSpec((1, D_MODEL), lambda i: (0, 0))
    seq_vec = pl.BlockSpec((None, 1, D_MODEL), lambda i: (i // per_seq, 0, 0))
    return rows, vec, seq_vec


def _norm_modulate(x, g, sh, sc):
    rows, vec, seq_vec = _row_specs(NORM_TM)
    return pl.pallas_call(
        _norm_mod_kernel,
        grid=(TOKENS // NORM_TM,),
        in_specs=[rows, vec, seq_vec, seq_vec],
        out_specs=rows,
        out_shape=jax.ShapeDtypeStruct((TOKENS, D_MODEL), BF16),
        compiler_params=_params(1),
        name="norm_mod",
    )(x, g.reshape(1, D_MODEL), sh, sc)


MM_TM = 1024
MM_TN = 512


def _cast_weights_once(w_refs, wb_refs):
    @pl.when(pl.program_id(1) == 0)
    def _():
        for w_ref, wb_ref in zip(w_refs, wb_refs):
            wb_ref[...] = w_ref[...].astype(BF16)


def _dot_nt(a, b_t):
    return lax.dot_general(a, b_t, (((1,), (1,)), ((), ())), preferred_element_type=F32)


def _mm_nt_kernel(a_ref, wt_ref, o_ref, wb_ref):
    _cast_weights_once((wt_ref,), (wb_ref,))
    o_ref[...] = _dot_nt(a_ref[...], wb_ref[...]).astype(o_ref.dtype)


def _matmul_nt(a, wt3, layer, n_out, out_dtype):
    k = a.shape[1]
    tn = _wide_col_tile(k)
    return pl.pallas_call(
        _mm_nt_kernel,
        grid=(n_out // tn, TOKENS // MM_TM),
        in_specs=[
            pl.BlockSpec((MM_TM, k), lambda j, i: (i, 0)),
            pl.BlockSpec((None, tn, k), lambda j, i: (layer, j, 0)),
        ],
        out_specs=pl.BlockSpec((MM_TM, tn), lambda j, i: (i, j)),
        out_shape=jax.ShapeDtypeStruct((TOKENS, n_out), out_dtype),
        scratch_shapes=[pltpu.VMEM((tn, k), BF16)],
        compiler_params=_params(2),
        name="matmul_nt",
    )(a, wt3)


RES_ROW_BYTES = 6 * 1024 * 1024
WIDE_TILE_BYTES = 12 * 1024 * 1024


def _wide_col_tile(k):
    return min(2 * MM_TN, WIDE_TILE_BYTES // (4 * k) // MXU_DIM * MXU_DIM)


def _mm_res_kernel(a_ref, w_ref, x_ref, g_ref, o_ref, wb_ref):
    _cast_weights_once((w_ref,), (wb_ref,))
    o_ref[...] = x_ref[...] + g_ref[...] * _dot(a_ref[...], wb_ref[...])


def _matmul_residual(a, w3, layer, x, gate):
    k = a.shape[1]
    tm = min(MM_TM, RES_ROW_BYTES // (2 * k) // MXU_DIM * MXU_DIM)
    tn = _wide_col_tile(k)
    per_seq = SEQ // tm
    return pl.pallas_call(
        _mm_res_kernel,
        grid=(D_MODEL // tn, TOKENS // tm),
        in_specs=[
            pl.BlockSpec((tm, k), lambda j, i: (i, 0)),
            pl.BlockSpec((None, k, tn), lambda j, i: (layer, 0, j)),
            pl.BlockSpec((tm, tn), lambda j, i: (i, j)),
            pl.BlockSpec((None, 1, tn), lambda j, i: (i // per_seq, 0, j)),
        ],
        out_specs=pl.BlockSpec((tm, tn), lambda j, i: (i, j)),
        out_shape=jax.ShapeDtypeStruct((TOKENS, D_MODEL), F32),
        scratch_shapes=[pltpu.VMEM((k, tn), BF16)],
        compiler_params=_params(2),
        name="matmul_residual",
    )(a, w3, x, gate)


CONV_TM = 1024


def _conv_in_kernel(a_ref, wgb_ref, wgc_ref, wu_ref, kc_ref, o_ref, wb_ref, v_ref):
    i = pl.program_id(1)
    _cast_weights_once((wgb_ref, wgc_ref, wu_ref), (wb_ref.at[0], wb_ref.at[1], wb_ref.at[2]))

    @pl.when(i % (SEQ // CONV_TM) == 0)
    def _():
        v_ref[0:SUBLANES, :] = jnp.zeros((SUBLANES, MM_TN), F32)

    a = a_ref[...]
    gc = _dot(a, wb_ref[1])
    u = _dot(a, wb_ref[2])
    v_ref[SUBLANES:SUBLANES + CONV_TM, :] = gc * u
    conv = kc_ref[2:3, :] * v_ref[SUBLANES:SUBLANES + CONV_TM, :]
    conv = conv + kc_ref[1:2, :] * v_ref[SUBLANES - 1:SUBLANES - 1 + CONV_TM, :]
    conv = conv + kc_ref[0:1, :] * v_ref[SUBLANES - 2:SUBLANES - 2 + CONV_TM, :]
    gb = _dot(a, wb_ref[0])
    o_ref[...] = (gb * conv).astype(o_ref.dtype)
    v_ref[0:SUBLANES, :] = v_ref[CONV_TM:CONV_TM + SUBLANES, :]


def _conv_in(h, conv_w_in, conv_k, layer):
    nb = D_MODEL // MM_TN
    w_spec = lambda off: pl.BlockSpec((None, D_MODEL, MM_TN), lambda j, i: (layer, 0, j + off))
    return pl.pallas_call(
        _conv_in_kernel,
        grid=(nb, TOKENS // CONV_TM),
        in_specs=[
            pl.BlockSpec((CONV_TM, D_MODEL), lambda j, i: (i, 0)),
            w_spec(0), w_spec(nb), w_spec(2 * nb),
            pl.BlockSpec((None, CONV_W, MM_TN), lambda j, i: (layer, 0, j)),
        ],
        out_specs=pl.BlockSpec((CONV_TM, MM_TN), lambda j, i: (i, j)),
        out_shape=jax.ShapeDtypeStruct((TOKENS, D_MODEL), BF16),
        scratch_shapes=[
            pltpu.VMEM((3, D_MODEL, MM_TN), BF16),
            pltpu.VMEM((CONV_TM + SUBLANES, MM_TN), F32),
        ],
        compiler_params=_params(2),
        name="conv_in",
    )(h, conv_w_in, conv_w_in, conv_w_in, conv_k)


def _ffn_up_kernel(a_ref, w1_ref, w3_ref, o_ref, wb_ref):
    _cast_weights_once((w1_ref, w3_ref), (wb_ref.at[0], wb_ref.at[1]))
    a = a_ref[...]
    p = _dot(a, wb_ref[0])
    q = _dot(a, wb_ref[1])
    o_ref[...] = (_silu(p) * q).astype(o_ref.dtype)


def _ffn_up(h, w13, layer):
    nb = D_FF // MM_TN
    w_spec = lambda off: pl.BlockSpec((None, D_MODEL, MM_TN), lambda j, i: (layer, 0, j + off))
    return pl.pallas_call(
        _ffn_up_kernel,
        grid=(nb, TOKENS // MM_TM),
        in_specs=[pl.BlockSpec((MM_TM, D_MODEL), lambda j, i: (i, 0)), w_spec(0), w_spec(nb)],
        out_specs=pl.BlockSpec((MM_TM, MM_TN), lambda j, i: (i, j)),
        out_shape=jax.ShapeDtypeStruct((TOKENS, D_FF), BF16),
        scratch_shapes=[pltpu.VMEM((2, D_MODEL, MM_TN), BF16)],
        compiler_params=_params(2),
        name="ffn_up",
    )(h, w13, w13)


LOGA_TM = 512


def _loga_kernel(h_ref, wl_ref, wgk_ref, bgk_ref, o_ref):
    low = _dot_nt(h_ref[...], wl_ref[...].astype(BF16))
    z =_dot(low.astype(BF16), wgk_ref[...].astype(BF16)) + bgk_ref[...]
    log_sig = jnp.minimum(z, 0.0) - jnp.log1p(jnp.exp(-jnp.abs(z)))
    o_ref[...] = log_sig * (1.0 / GLA_GATE_NORM)


def _gla_log_decay(h, w_in_t, layer, w_gk, b_gk):
    return pl.pallas_call(
        _loga_kernel,
        grid=(TOKENS // LOGA_TM,),
        in_specs=[
            pl.BlockSpec((LOGA_TM, D_MODEL), lambda i: (i, 0)),
            pl.BlockSpec((None, GLA_GATE_RANK, D_MODEL), lambda i: (layer, GLA_MAIN // GLA_GATE_RANK, 0)),
            pl.BlockSpec((GLA_GATE_RANK, GLA_DK), lambda i: (0, 0)),
            pl.BlockSpec((1, GLA_DK), lambda i: (0, 0)),
        ],
        out_specs=pl.BlockSpec((LOGA_TM, GLA_DK), lambda i: (i, 0)),
        out_shape=jax.ShapeDtypeStruct((TOKENS, GLA_DK), F32),
        compiler_params=_params(1),
        name="gla_log_decay",
    )(h, w_in_t, w_gk, b_gk.reshape(1, GLA_DK))


GLA_ROWS = 512


def _gla_kernel(q_ref, k_ref, v_ref, g_ref, la_ref, ng_ref, o_ref, st_ref):
    @pl.when(pl.program_id(1) == 0)
    def _():
        st_ref[...] = jnp.zeros(st_ref.shape, F32)

    row = lax.broadcasted_iota(jnp.int32, (CHUNK, CHUNK), 0)
    col = lax.broadcasted_iota(jnp.int32, (CHUNK, CHUNK), 1)
    tri = (col <= row).astype(BF16)

    def chunk(ci, carry):
        rows = pl.ds(pl.multiple_of(ci * CHUNK, CHUNK), CHUNK)
        la = la_ref[rows, :]
        la_hi = la.astype(BF16)
        la_lo = (la - la_hi.astype(F32)).astype(BF16)
        bcum = _dot(tri, la_hi) + _dot(tri, la_lo)
        btot = bcum[CHUNK - 1:CHUNK, :]
        k_dec = (k_ref[rows, :].astype(F32) * jnp.exp(btot - bcum)).astype(BF16)
        decay = jnp.exp(btot)
        q = (q_ref[rows, :].astype(F32) * (GLA_DK_HEAD ** -0.5)).astype(BF16)
        for h in range(GLA_HEADS):
            kcols = slice(h * GLA_DK_HEAD, (h + 1) * GLA_DK_HEAD)
            vcols = slice(h * GLA_DV_HEAD, (h + 1) * GLA_DV_HEAD)
            kv_t = lax.dot_general(v_ref[rows, vcols], k_dec[:, kcols],
                                   (((0,), (0,)), ((), ())), preferred_element_type=F32)
            st = st_ref[h] * decay[:, kcols] + kv_t
            st_ref[h] = st
            o = lax.dot_general(q[:, kcols], st.astype(BF16),
                                (((1,), (1,)), ((), ())), preferred_element_type=F32)
            o = o * lax.rsqrt(jnp.mean(o * o, axis=-1, keepdims=True) + EPS) * ng_ref[...]
            o = o * _silu(g_ref[rows, vcols].astype(F32))
            o_ref[rows, vcols] = o.astype(o_ref.dtype)
        return carry

    lax.fori_loop(0, GLA_ROWS // CHUNK, chunk, 0)


def _gla_scan(proj, log_a, norm_g):
    per_seq = SEQ // GLA_ROWS
    rows = lambda width, blk: pl.BlockSpec((GLA_ROWS, width), lambda b, s: (b * per_seq + s, blk))
    return pl.pallas_call(
        _gla_kernel,
        grid=(BATCH, per_seq),
        in_specs=[
            rows(GLA_DK, 0), rows(GLA_DK, 1), rows(GLA_DV, 1), rows(GLA_DV, 2),
            rows(GLA_DK, 0),
            pl.BlockSpec((1, GLA_DV_HEAD), lambda b, s: (0, 0)),
        ],
        out_specs=rows(GLA_DV, 0),
        out_shape=jax.ShapeDtypeStruct((TOKENS, GLA_DV), BF16),
        scratch_shapes=[pltpu.VMEM((GLA_HEADS, GLA_DV_HEAD, GLA_DK_HEAD), F32)],
        compiler_params=_params(2),
        name="gla_scan",
    )(proj, proj, proj, proj, log_a, norm_g.reshape(1, GLA_DV_HEAD))


ROUTE_TM = 512
META_E0, META_E1, META_G0, META_G1, META_R0, META_R1 = range(6)


def _route_kernel(x_ref, g_ref, sh_ref, sc_ref, r_ref, h_ref, meta_ref, cnt_ref, run_ref):
    @pl.when(pl.program_id(0) == 0)
    def _():
        run_ref[...] = jnp.zeros(run_ref.shape, F32)

    h = _norm_mod(x_ref[...], g_ref[...], sc_ref[...], sh_ref[...])
    h_ref[...] = h
    logits =jnp.dot(h, r_ref[...], preferred_element_type=F32, precision=lax.Precision.HIGHEST)
    lane = lax.broadcasted_iota(jnp.int32, logits.shape, 1)
    logits = jnp.where(lane < N_EXPERTS, logits, -jnp.inf)
    m0 = jnp.max(logits, axis=1, keepdims=True)
    e0 = jnp.min(jnp.where(logits == m0, lane, LANES), axis=1, keepdims=True)
    rest = jnp.where(lane == e0, -jnp.inf, logits)
    m1 = jnp.max(rest, axis=1, keepdims=True)
    e1 = jnp.min(jnp.where(rest == m1, lane, LANES), axis=1, keepdims=True)
    p = jnp.exp(m1 - m0)
    gate0 = 1.0 / (1.0 + p)
    gate1 = p / (1.0 + p)

    hot0 = (lane == e0).astype(F32)
    hot1 = (lane == e1).astype(F32)
    both = hot0 + hot1
    row = lax.broadcasted_iota(jnp.int32, (ROUTE_TM, ROUTE_TM), 0)
    col = lax.broadcasted_iota(jnp.int32, (ROUTE_TM, ROUTE_TM), 1)
    before = _dot((col < row).astype(BF16), both.astype(BF16)) + run_ref[0:1, :]
    rank0 = jnp.sum(hot0 * before, axis=1, keepdims=True)
    rank1 = jnp.sum(hot1 * before, axis=1, keepdims=True)
    run_ref[...] = run_ref[...] + jnp.sum(both, axis=0, keepdims=True)
    cnt_ref[...] = run_ref[...]

    meta = jnp.zeros(logits.shape, F32)
    for lane_id, val in ((META_E0, e0.astype(F32)), (META_E1, e1.astype(F32)), (META_G0, gate0),
                         (META_G1, gate1), (META_R0, rank0), (META_R1, rank1)):
        meta = jnp.where(lane == lane_id, val, meta)
    meta_ref[...] = meta


def _route(x, g, sh, sc, router):
    rows, vec, seq_vec = _row_specs(ROUTE_TM)
    router = jnp.pad(router, ((0, 0), (0, LANES - N_EXPERTS)))
    return pl.pallas_call(
        _route_kernel,
        grid=(TOKENS // ROUTE_TM,),
        in_specs=[rows, vec, seq_vec, seq_vec, pl.BlockSpec((D_MODEL, LANES), lambda i: (0, 0))],
        out_specs=[
            rows,
            pl.BlockSpec((ROUTE_TM, LANES), lambda i: (i, 0)),
            pl.BlockSpec((SUBLANES, LANES), lambda i: (0, 0)),
        ],
        out_shape=[
            jax.ShapeDtypeStruct((TOKENS, D_MODEL), F32),
            jax.ShapeDtypeStruct((TOKENS, LANES), F32),
            jax.ShapeDtypeStruct((SUBLANES, LANES), F32),
        ],
        scratch_shapes=[pltpu.VMEM((SUBLANES, LANES), F32)],
        compiler_params=_params(1),
        name="moe_route",
    )(x, g.reshape(1, D_MODEL), sh, sc, router)


def _row_copy(src_hbm, src_row, dst_groups, group, sub, sem):
    return pltpu.make_async_copy(
        src_hbm.at[pl.ds(src_row, 1), :], dst_groups.at[group, pl.ds(sub, 1), :], sem)


def _for_rows(n_rows, fn):
    def body(group, carry):
        for sub in range(SUBLANES):
            fn(group, sub)
        return carry

    lax.fori_loop(0, n_rows // SUBLANES, body, 0)


def _gather_kernel(tok_ref, h_hbm, o_ref, buf_ref, sem):
    b = pl.program_id(0)

    def issue(blk):
        slot = blk & 1
        _for_rows(MOE_ROWS, lambda group, sub: _row_copy(
            h_hbm, tok_ref[blk * MOE_ROWS + group * SUBLANES + sub], buf_ref.at[slot], group, sub,
            sem.at[slot]).start())

    @pl.when(b == 0)
    def _():
        issue(b)

    @pl.when(b + 1 < pl.num_programs(0))
    def _():
        issue(b + 1)

    slot = b & 1
    _for_rows(MOE_ROWS, lambda group, sub: _row_copy(
        h_hbm, 0, buf_ref.at[slot], group, sub, sem.at[slot]).wait())
    o_ref[...] = buf_ref[slot].reshape(MOE_ROWS, D_MODEL).astype(o_ref.dtype)


def _gather_slots(slot_tok, h):
    return pl.pallas_call(
        _gather_kernel,
        grid_spec=pltpu.PrefetchScalarGridSpec(
            num_scalar_prefetch=1,
            grid=(N_SLOT_BLOCKS,),
            in_specs=[pl.BlockSpec(memory_space=pl.ANY)],
            out_specs=pl.BlockSpec((MOE_ROWS, D_MODEL), lambda b, tok: (b, 0)),
            scratch_shapes=[pltpu.VMEM((2, MOE_ROWS // SUBLANES, SUBLANES, D_MODEL), F32),
                            pltpu.SemaphoreType.DMA((2,))],
        ),
        out_shape=jax.ShapeDtypeStruct((N_SLOTS, D_MODEL), BF16),
        compiler_params=_params(1),
        name="moe_gather",
    )(slot_tok, h)


BLOCK_DMA_PRIORITY = 1
WEIGHT_DMA_PRIORITY = 1
CHUNK_BLOCKS = 2


def _grouped_kernel(layer, col_offsets, compute, first_ref, cnt_ref, a_hbm, w_hbm, o_hbm,
                    wb_ref, w_buf, a_buf, o_buf, sem_w, sem_in, sem_out, pend_ref):
    f, e = pl.program_id(0), pl.program_id(1)
    n, first = cnt_ref[e], first_ref[e]
    col = pl.multiple_of(f * MM_TN, MM_TN)
    n_full = n // CHUNK_BLOCKS
    has_tail = n % CHUNK_BLOCKS == 1
    tail_blk = first + CHUNK_BLOCKS * n_full
    tail_slot = n_full & 1
    last_e = e == N_EXPERTS - 1
    is_first_step = jnp.logical_and(f == 0, e == 0)
    is_last_step = jnp.logical_and(last_e, f == pl.num_programs(0) - 1)
    e_next = jnp.where(last_e, 0, e + 1)
    f_next = jnp.where(last_e, f + 1, f)
    w_slot = (f * N_EXPERTS + e) & 1

    def weight_copies(tile, expert, slot):
        return [pltpu.make_async_copy(
            w_hbm.at[layer, expert, :, pl.ds(pl.multiple_of(tile * MM_TN + off, MM_TN), MM_TN)],
            w_buf.at[slot, k], sem_w.at[slot]) for k, off in enumerate(col_offsets)]

    def copy_in(blk, n_blocks, slot):
        rows = n_blocks * MOE_ROWS
        return pltpu.make_async_copy(
            a_hbm.at[pl.ds(blk * MOE_ROWS, rows), :], a_buf.at[slot, pl.ds(0, rows), :], sem_in.at[slot])

    def copy_out(blk, n_blocks, slot):
        rows = n_blocks * MOE_ROWS
        return pltpu.make_async_copy(
            o_buf.at[slot, pl.ds(0, rows), :],
            o_hbm.at[pl.ds(blk * MOE_ROWS, rows), pl.ds(col, MM_TN)], sem_out.at[slot])

    def start_first_chunk(expert):
        @pl.when(cnt_ref[expert] >= CHUNK_BLOCKS)
        def _():
            copy_in(first_ref[expert], CHUNK_BLOCKS, 0).start()

        @pl.when(cnt_ref[expert] == 1)
        def _():
            copy_in(first_ref[expert], 1, 0).start()

    def wait_out(slot):
        for n_blocks in (CHUNK_BLOCKS, 1):
            @pl.when(pend_ref[slot] == n_blocks)
            def _():
                copy_out(0, n_blocks, slot).wait()

        pend_ref[slot] = 0

    def start_out(blk, n_blocks, slot):
        copy_out(blk, n_blocks, slot).start()
        pend_ref[slot] = n_blocks

    @pl.when(is_first_step)
    def _():
        pend_ref[0] = 0
        pend_ref[1] = 0
        for cp in weight_copies(f, e, w_slot):
            cp.start()
        start_first_chunk(e)

    for cp in weight_copies(f, e, w_slot):
        cp.wait()
    for k in range(len(col_offsets)):
        wb_ref[k] = w_buf[w_slot, k].astype(BF16)

    @pl.when(jnp.logical_not(is_last_step))
    def _():
        for cp in weight_copies(f_next, e_next, 1 - w_slot):
            cp.start(priority=WEIGHT_DMA_PRIORITY)

    def full_chunk(c, carry):
        slot = c & 1
        blk = first + CHUNK_BLOCKS * c
        copy_in(blk, CHUNK_BLOCKS, slot).wait()

        @pl.when(c + 1 < n_full)
        def _():
            copy_in(blk + CHUNK_BLOCKS, CHUNK_BLOCKS, 1 - slot).start()

        @pl.when(jnp.logical_and(c + 1 == n_full, has_tail))
        def _():
            copy_in(blk + CHUNK_BLOCKS, 1, 1 - slot).start()

        wait_out(slot)
        o_buf[slot] = compute(a_buf[slot], wb_ref).astype(o_buf.dtype)
        start_out(blk, CHUNK_BLOCKS, slot)
        return carry

    lax.fori_loop(0, n_full, full_chunk, 0)

    @pl.when(has_tail)
    def _():
        copy_in(tail_blk, 1, tail_slot).wait()
        wait_out(tail_slot)
        o_buf[tail_slot, 0:MOE_ROWS, :] = compute(
            a_buf[tail_slot, 0:MOE_ROWS, :], wb_ref).astype(o_buf.dtype)
        start_out(tail_blk, 1, tail_slot)

    @pl.when(jnp.logical_not(is_last_step))
    def _():
        start_first_chunk(e_next)

    @pl.when(last_e)
    def _():
        wait_out(0)
        o_buf[0, 0:MOE_ROWS, :] = jnp.zeros((MOE_ROWS, MM_TN), o_buf.dtype)

        def zero_block(blk, carry):
            copy_out(blk, 1, 0).start()
            copy_out(blk, 1, 0).wait()
            return carry

        lax.fori_loop(first + n, N_SLOT_BLOCKS, zero_block, 0)

    @pl.when(is_last_step)
    def _():
        wait_out(0)
        wait_out(1)


def _grouped_matmul(name, compute, first_blk, n_blk, a, w4, layer, col_offsets, n_out, out_dtype):
    k = a.shape[1]
    n_w = len(col_offsets)
    return pl.pallas_call(
        functools.partial(_grouped_kernel, layer, col_offsets, compute),
        grid_spec=pltpu.PrefetchScalarGridSpec(
            num_scalar_prefetch=2,
            grid=(n_out // MM_TN, N_EXPERTS),
            in_specs=[pl.BlockSpec(memory_space=pl.ANY), pl.BlockSpec(memory_space=pl.ANY)],
            out_specs=pl.BlockSpec(memory_space=pl.ANY),
            scratch_shapes=[
                pltpu.VMEM((n_w, k, MM_TN), BF16),
                pltpu.VMEM((2, n_w, k, MM_TN), F32),
                pltpu.VMEM((2, CHUNK_BLOCKS * MOE_ROWS, k), BF16),
                pltpu.VMEM((2, CHUNK_BLOCKS * MOE_ROWS, MM_TN), out_dtype),
                pltpu.SemaphoreType.DMA((2,)),
                pltpu.SemaphoreType.DMA((2,)),
                pltpu.SemaphoreType.DMA((2,)),
                pltpu.SMEM((2,), jnp.int32),
            ],
        ),
        out_shape=jax.ShapeDtypeStruct((N_SLOTS, n_out), out_dtype),
        compiler_params=_params(2),
        name=name,
    )(first_blk, n_blk, a, w4)


def _swiglu_tile(a, wb_ref):
    return _silu(_dot(a, wb_ref[0])) * _dot(a, wb_ref[1])


def _down_tile(a, wb_ref):
    return _dot(a, wb_ref[0])


COMB_TM = 256


def _combine_kernel(final, dest_ref, x_ref, meta_ref, g2_ref, y_hbm, *refs):
    if final:
        fg_ref, o_ref, buf_ref, sem = refs
    else:
        ng_ref, nsh_ref, nsc_ref, o_ref, h_ref, buf_ref, sem = refs
    i = pl.program_id(0)

    def issue(tile):
        slot = tile & 1
        for k in range(TOP_K):
            _for_rows(COMB_TM, lambda group, sub: _row_copy(
                y_hbm, dest_ref[TOP_K * (tile * COMB_TM + group * SUBLANES + sub) + k],
                buf_ref.at[slot, k], group, sub, sem.at[slot]).start(priority=BLOCK_DMA_PRIORITY))

    @pl.when(i == 0)
    def _():
        issue(i)

    @pl.when(i + 1 < pl.num_programs(0))
    def _():
        issue(i + 1)

    slot = i & 1
    for k in range(TOP_K):
        _for_rows(COMB_TM, lambda group, sub: _row_copy(
            y_hbm, 0, buf_ref.at[slot, k], group, sub, sem.at[slot]).wait())
    meta = meta_ref[...]
    y0 = buf_ref[slot, 0].reshape(COMB_TM, D_MODEL)
    y1 = buf_ref[slot, 1].reshape(COMB_TM, D_MODEL)
    f = meta[:, META_G0:META_G0 + 1] * y0 + meta[:, META_G1:META_G1 + 1] * y1
    x_new = x_ref[...] + g2_ref[...] * f
    if final:
        ms = jnp.mean(x_new * x_new, axis=-1, keepdims=True)
        o_ref[...] = x_new * lax.rsqrt(ms + EPS) * fg_ref[...]
    else:
        o_ref[...] = x_new
        h_ref[...] = _norm_mod(x_new, ng_ref[...], nsc_ref[...], nsh_ref[...]).astype(h_ref.dtype)


def _combine(dest_flat, x, meta, g2, y_slots, final_g=None, next_norm=None):
    final = next_norm is None
    per_seq = SEQ // COMB_TM
    rows = pl.BlockSpec((COMB_TM, D_MODEL), lambda i, d: (i, 0))
    vec = pl.BlockSpec((1, D_MODEL), lambda i, d: (0, 0))
    seq_vec = pl.BlockSpec((None, 1, D_MODEL), lambda i, d: (i // per_seq, 0, 0))
    if final:
        extra_in, extra_specs = (final_g.reshape(1, D_MODEL),), [vec]
        out_specs, out_shape = rows, jax.ShapeDtypeStruct((TOKENS, D_MODEL), F32)
    else:
        g, sh, sc = next_norm
        extra_in, extra_specs = (g.reshape(1, D_MODEL), sh, sc), [vec, seq_vec, seq_vec]
        out_specs = [rows, rows]
        out_shape = [jax.ShapeDtypeStruct((TOKENS, D_MODEL), F32),
                     jax.ShapeDtypeStruct((TOKENS, D_MODEL), BF16)]
    return pl.pallas_call(
        functools.partial(_combine_kernel, final),
        grid_spec=pltpu.PrefetchScalarGridSpec(
            num_scalar_prefetch=1,
            grid=(TOKENS // COMB_TM,),
            in_specs=[rows, pl.BlockSpec((COMB_TM, LANES), lambda i, d: (i, 0)), seq_vec,
                      pl.BlockSpec(memory_space=pl.ANY)] + extra_specs,
            out_specs=out_specs,
            scratch_shapes=[pltpu.VMEM((2, TOP_K, COMB_TM // SUBLANES, SUBLANES, D_MODEL), F32),
                            pltpu.SemaphoreType.DMA((2,))],
        ),
        out_shape=out_shape,
        compiler_params=_params(1),
        name="moe_combine",
    )(dest_flat, x, meta, g2, y_slots, *extra_in)


def _slot_plan(meta, counts):
    top_e = meta[:, META_E0:META_E1 + 1].astype(jnp.int32)
    rank = meta[:, META_R0:META_R1 + 1].astype(jnp.int32)
    counts = counts[0, :N_EXPERTS].astype(jnp.int32)
    n_blk = (counts + MOE_ROWS - 1) // MOE_ROWS
    first_blk = jnp.cumsum(n_blk) - n_blk
    dest = (first_blk[top_e] * MOE_ROWS + rank).reshape(TOKENS * TOP_K)
    pair_tok = jnp.arange(TOKENS * TOP_K, dtype=jnp.int32) // TOP_K
    slot_tok = jnp.zeros((N_SLOTS,), jnp.int32).at[dest].set(pair_tok)
    return dest, slot_tok, first_blk.astype(jnp.int32), n_blk.astype(jnp.int32)


def kernel(x, c, ada_w, ada_b, norm_g, conv_w_in, conv_k, conv_w_out, gla_w_in, gla_w_gk, gla_b_gk,
           gla_norm_g, gla_w_out, ffn_w13, ffn_w2, moe_router, moe_w13, moe_w2, final_g):
    assert x.shape == (BATCH, SEQ, D_MODEL) and x.dtype == F32
    assert DEPTH % 2 == 0
    mod = _ada_all(c, ada_w, ada_b)[:, :BATCH]
    xt = x.reshape(TOKENS, D_MODEL)
    gla_w_in_t = jnp.swapaxes(gla_w_in, 1, 2)
    mods = [[mod[i, :, n * D_MODEL:(n + 1) * D_MODEL].reshape(BATCH, 1, D_MODEL) for n in range(6)]
            for i in range(DEPTH)]
    h = None
    for i in range(DEPTH):
        j = i // 2
        sh1, sc1, g1, sh2, sc2, g2 = mods[i]
        if h is None:
            h = _norm_modulate(xt, norm_g[i, 0], sh1, sc1)
        if i % 2 == 0:
            y = _conv_in(h, conv_w_in, conv_k, j)
            xt = _matmul_residual(y, conv_w_out, j, xt, g1)
            h = _norm_modulate(xt, norm_g[i, 1], sh2, sc2)
            t = _ffn_up(h, ffn_w13, j)
            xt = _matmul_residual(t, ffn_w2, j, xt, g2)
            h = None
        else:
            proj = _matmul_nt(h, gla_w_in_t, j, GLA_MAIN, BF16)
            log_a = _gla_log_decay(h, gla_w_in_t, j, gla_w_gk[j], gla_b_gk[j])
            o = _gla_scan(proj, log_a, gla_norm_g[j])
            xt = _matmul_residual(o, gla_w_out, j, xt, g1)
            h32, meta, counts = _route(xt, norm_g[i, 1], sh2, sc2, moe_router[j])
            dest, slot_tok, first_blk, n_blk = _slot_plan(meta, counts)
            xs = _gather_slots(slot_tok, h32)
            t = _grouped_matmul("moe_up", _swiglu_tile, first_blk, n_blk, xs, moe_w13, j,
                                (0, D_FF), D_FF, BF16)
            y = _grouped_matmul("moe_down", _down_tile, first_blk, n_blk, t, moe_w2, j,
                                (0,), D_MODEL, F32)
            if i == DEPTH - 1:
                xt = _combine(dest, xt, meta, g2, y, final_g=final_g)
            else:
                nsh, nsc = mods[i + 1][0], mods[i + 1][1]
                xt, h = _combine(dest, xt, meta, g2, y, next_norm=(norm_g[i + 1, 0], nsh, nsc))
    return xt.reshape(BATCH, SEQ, D_MODEL)
```

```python
import functools

import jax
import jax.numpy as jnp
from jax import lax
from jax.experimental import pallas as pl
from jax.experimental.pallas import tpu as pltpu

D_MODEL = 2048
BATCH = 4
SEQ = 2048
TOKENS = BATCH * SEQ
DEPTH = 4
CHUNK = 64
EPS = 1e-6
CONV_W = 3
GLA_HEADS = 4
GLA_DK = D_MODEL // 2
GLA_DV = D_MODEL
GLA_DK_HEAD = GLA_DK // GLA_HEADS
GLA_DV_HEAD = GLA_DV // GLA_HEADS
GLA_GATE_RANK = 16
GLA_GATE_NORM = 16.0
GLA_MAIN = 2 * GLA_DK + 2 * GLA_DV
D_FF = 5632
N_EXPERTS = 8
TOP_K = 2

LANES = 128
SUBLANES = 8
MXU_DIM = 256
VMEM_LIMIT = 56 * 1024 * 1024

MOE_ROWS = 256
N_SLOT_BLOCKS = TOKENS * TOP_K // MOE_ROWS + N_EXPERTS
N_SLOTS = N_SLOT_BLOCKS * MOE_ROWS

F32 = jnp.float32
BF16 = jnp.bfloat16


def _params(n_axes):
    return pltpu.CompilerParams(
        dimension_semantics=("arbitrary",) * n_axes, vmem_limit_bytes=VMEM_LIMIT)


def _dot(a, b):
    return jnp.dot(a, b, preferred_element_type=F32)


def _silu(v):
    return v * jax.nn.sigmoid(v)


ADA_TN = 1024


def _ada_kernel(c_ref, w_ref, b_ref, o_ref):
    c_act = _silu(c_ref[...]).astype(BF16)
    o_ref[...] = _dot(c_act, w_ref[...].astype(BF16)) + b_ref[...]


def _ada_all(c, ada_w, ada_b):
    c_pad = jnp.pad(c, ((0, SUBLANES - BATCH), (0, 0)))
    n = 6 * D_MODEL
    return pl.pallas_call(
        _ada_kernel,
        grid=(DEPTH, n // ADA_TN),
        in_specs=[
            pl.BlockSpec((SUBLANES, D_MODEL), lambda l, j: (0, 0)),
            pl.BlockSpec((None, D_MODEL, ADA_TN), lambda l, j: (l, 0, j)),
            pl.BlockSpec((None, 1, ADA_TN), lambda l, j: (l, 0, j)),
        ],
        out_specs=pl.BlockSpec((None, SUBLANES, ADA_TN), lambda l, j: (l, 0, j)),
        out_shape=jax.ShapeDtypeStruct((DEPTH, SUBLANES, n), F32),
        compiler_params=_params(2),
        name="ada_mod",
    )(c_pad, ada_w, ada_b.reshape(DEPTH, 1, n))


NORM_TM = 512


def _norm_mod(x, g, sc, sh):
    ms = jnp.mean(x * x, axis=-1, keepdims=True)
    y = x * lax.rsqrt(ms + EPS) * g
    return y * (1.0 + sc) + sh


def _norm_mod_kernel(x_ref, g_ref, sh_ref, sc_ref, h_ref):
    h_ref[...] = _norm_mod(x_ref[...], g_ref[...], sc_ref[...], sh_ref[...]).astype(h_ref.dtype)


def _row_specs(tm):
    per_seq = SEQ // tm
    rows = pl.BlockSpec((tm, D_MODEL), lambda i: (i, 0))
    vec = pl.BlockSpec((1, D_MODEL), lambda i: (0, 0))
    seq_vec = pl.BlockSpec((None, 1, D_MODEL), lambda i: (i // per_seq, 0, 0))
    return rows, vec, seq_vec


def _norm_modulate(x, g, sh, sc):
    rows, vec, seq_vec = _row_specs(NORM_TM)
    return pl.pallas_call(
        _norm_mod_kernel,
        grid=(TOKENS // NORM_TM,),
        in_specs=[rows, vec, seq_vec, seq_vec],
        out_specs=rows,
        out_shape=jax.ShapeDtypeStruct((TOKENS, D_MODEL), BF16),
        compiler_params=_params(1),
        name="norm_mod",
    )(x, g.reshape(1, D_MODEL), sh, sc)


MM_TM = 1024
MM_TN = 512


def _cast_weights_once(w_refs, wb_refs):
    @pl.when(pl.program_id(1) == 0)
    def _():
        for w_ref, wb_ref in zip(w_refs, wb_refs):
            wb_ref[...] = w_ref[...].astype(BF16)


def _dot_nt(a, b_t):
    return lax.dot_general(a, b_t, (((1,), (1,)), ((), ())), preferred_element_type=F32)


def _mm_nt_kernel(a_ref, wt_ref, o_ref, wb_ref):
    _cast_weights_once((wt_ref,), (wb_ref,))
    o_ref[...] = _dot_nt(a_ref[...], wb_ref[...]).astype(o_ref.dtype)


def _matmul_nt(a, wt3, layer, n_out, out_dtype):
    k = a.shape[1]
    tn = _wide_col_tile(k)
    return pl.pallas_call(
        _mm_nt_kernel,
        grid=(n_out // tn, TOKENS // MM_TM),
        in_specs=[
            pl.BlockSpec((MM_TM, k), lambda j, i: (i, 0)),
            pl.BlockSpec((None, tn, k), lambda j, i: (layer, j, 0)),
        ],
        out_specs=pl.BlockSpec((MM_TM, tn), lambda j, i: (i, j)),
        out_shape=jax.ShapeDtypeStruct((TOKENS, n_out), out_dtype),
        scratch_shapes=[pltpu.VMEM((tn, k), BF16)],
        compiler_params=_params(2),
        name="matmul_nt",
    )(a, wt3)


RES_ROW_BYTES = 6 * 1024 * 1024
WIDE_TILE_BYTES = 12 * 1024 * 1024


def _wide_col_tile(k):
    return min(2 * MM_TN, WIDE_TILE_BYTES // (4 * k) // MXU_DIM * MXU_DIM)


def _mm_res_kernel(a_ref, w_ref, x_ref, g_ref, o_ref, wb_ref):
    _cast_weights_once((w_ref,), (wb_ref,))
    o_ref[...] = x_ref[...] + g_ref[...] * _dot(a_ref[...], wb_ref[...])


def _matmul_residual(a, w3, layer, x, gate):
    k = a.shape[1]
    tm = min(MM_TM, RES_ROW_BYTES // (2 * k) // MXU_DIM * MXU_DIM)
    tn = _wide_col_tile(k)
    per_seq = SEQ // tm
    return pl.pallas_call(
        _mm_res_kernel,
        grid=(D_MODEL // tn, TOKENS // tm),
        in_specs=[
            pl.BlockSpec((tm, k), lambda j, i: (i, 0)),
            pl.BlockSpec((None, k, tn), lambda j, i: (layer, 0, j)),
            pl.BlockSpec((tm, tn), lambda j, i: (i, j)),
            pl.BlockSpec((None, 1, tn), lambda j, i: (i // per_seq, 0, j)),
        ],
        out_specs=pl.BlockSpec((tm, tn), lambda j, i: (i, j)),
        out_shape=jax.ShapeDtypeStruct((TOKENS, D_MODEL), F32),
        scratch_shapes=[pltpu.VMEM((k, tn), BF16)],
        compiler_params=_params(2),
        name="matmul_residual",
    )(a, w3, x, gate)


CONV_TM = 1024


def _conv_in_kernel(a_ref, wgb_ref, wgc_ref, wu_ref, kc_ref, o_ref, wb_ref, v_ref):
    i = pl.program_id(1)
    _cast_weights_once((wgb_ref, wgc_ref, wu_ref), (wb_ref.at[0], wb_ref.at[1], wb_ref.at[2]))

    @pl.when(i % (SEQ // CONV_TM) == 0)
    def _():
        v_ref[0:SUBLANES, :] = jnp.zeros((SUBLANES, MM_TN), F32)

    a = a_ref[...]
    gc = _dot(a, wb_ref[1])
    u = _dot(a, wb_ref[2])
    v_ref[SUBLANES:SUBLANES + CONV_TM, :] = gc * u
    conv = kc_ref[2:3, :] * v_ref[SUBLANES:SUBLANES + CONV_TM, :]
    conv = conv + kc_ref[1:2, :] * v_ref[SUBLANES - 1:SUBLANES - 1 + CONV_TM, :]
    conv = conv + kc_ref[0:1, :] * v_ref[SUBLANES - 2:SUBLANES - 2 + CONV_TM, :]
    gb = _dot(a, wb_ref[0])
    o_ref[...] = (gb * conv).astype(o_ref.dtype)
    v_ref[0:SUBLANES, :] = v_ref[CONV_TM:CONV_TM + SUBLANES, :]


def _conv_in(h, conv_w_in, conv_k, layer):
    nb = D_MODEL // MM_TN
    w_spec = lambda off: pl.BlockSpec((None, D_MODEL, MM_TN), lambda j, i: (layer, 0, j + off))
    return pl.pallas_call(
        _conv_in_kernel,
        grid=(nb, TOKENS // CONV_TM),
        in_specs=[
            pl.BlockSpec((CONV_TM, D_MODEL), lambda j, i: (i, 0)),
            w_spec(0), w_spec(nb), w_spec(2 * nb),
            pl.BlockSpec((None, CONV_W, MM_TN), lambda j, i: (layer, 0, j)),
        ],
        out_specs=pl.BlockSpec((CONV_TM, MM_TN), lambda j, i: (i, j)),
        out_shape=jax.ShapeDtypeStruct((TOKENS, D_MODEL), BF16),
        scratch_shapes=[
            pltpu.VMEM((3, D_MODEL, MM_TN), BF16),
            pltpu.VMEM((CONV_TM + SUBLANES, MM_TN), F32),
        ],
        compiler_params=_params(2),
        name="conv_in",
    )(h, conv_w_in, conv_w_in, conv_w_in, conv_k)


def _ffn_up_kernel(a_ref, w1_ref, w3_ref, o_ref, wb_ref):
    _cast_weights_once((w1_ref, w3_ref), (wb_ref.at[0], wb_ref.at[1]))
    a = a_ref[...]
    p = _dot(a, wb_ref[0])
    q = _dot(a, wb_ref[1])
    o_ref[...] = (_silu(p) * q).astype(o_ref.dtype)


def _ffn_up(h, w13, layer):
    nb = D_FF // MM_TN
    w_spec = lambda off: pl.BlockSpec((None, D_MODEL, MM_TN), lambda j, i: (layer, 0, j + off))
    return pl.pallas_call(
        _ffn_up_kernel,
        grid=(nb, TOKENS // MM_TM),
        in_specs=[pl.BlockSpec((MM_TM, D_MODEL), lambda j, i: (i, 0)), w_spec(0), w_spec(nb)],
        out_specs=pl.BlockSpec((MM_TM, MM_TN), lambda j, i: (i, j)),
        out_shape=jax.ShapeDtypeStruct((TOKENS, D_FF), BF16),
        scratch_shapes=[pltpu.VMEM((2, D_MODEL, MM_TN), BF16)],
        compiler_params=_params(2),
        name="ffn_up",
    )(h, w13, w13)


LOGA_TM = 512


def _loga_kernel(h_ref, wl_ref, wgk_ref, bgk_ref, o_ref):
    low = _dot_nt(h_ref[...], wl_ref[...].astype(BF16))
    z =_dot(low.astype(BF16), wgk_ref[...].astype(BF16)) + bgk_ref[...]
    log_sig = jnp.minimum(z, 0.0) - jnp.log1p(jnp.exp(-jnp.abs(z)))
    o_ref[...] = log_sig * (1.0 / GLA_GATE_NORM)


def _gla_log_decay(h, w_in_t, layer, w_gk, b_gk):
    return pl.pallas_call(
        _loga_kernel,
        grid=(TOKENS // LOGA_TM,),
        in_specs=[
            pl.BlockSpec((LOGA_TM, D_MODEL), lambda i: (i, 0)),
            pl.BlockSpec((None, GLA_GATE_RANK, D_MODEL), lambda i: (layer, GLA_MAIN // GLA_GATE_RANK, 0)),
            pl.BlockSpec((GLA_GATE_RANK, GLA_DK), lambda i: (0, 0)),
            pl.BlockSpec((1, GLA_DK), lambda i: (0, 0)),
        ],
        out_specs=pl.BlockSpec((LOGA_TM, GLA_DK), lambda i: (i, 0)),
        out_shape=jax.ShapeDtypeStruct((TOKENS, GLA_DK), F32),
        compiler_params=_params(1),
        name="gla_log_decay",
    )(h, w_in_t, w_gk, b_gk.reshape(1, GLA_DK))


GLA_ROWS = 512


def _gla_kernel(q_ref, k_ref, v_ref, g_ref, la_ref, ng_ref, o_ref, st_ref):
    @pl.when(pl.program_id(1) == 0)
    def _():
        st_ref[...] = jnp.zeros(st_ref.shape, F32)

    row = lax.broadcasted_iota(jnp.int32, (CHUNK, CHUNK), 0)
    col = lax.broadcasted_iota(jnp.int32, (CHUNK, CHUNK), 1)
    tri = (col <= row).astype(BF16)

    def chunk(ci, carry):
        rows = pl.ds(pl.multiple_of(ci * CHUNK, CHUNK), CHUNK)
        la = la_ref[rows, :]
        la_hi = la.astype(BF16)
        la_lo = (la - la_hi.astype(F32)).astype(BF16)
        bcum = _dot(tri, la_hi) + _dot(tri, la_lo)
        btot = bcum[CHUNK - 1:CHUNK, :]
        k_dec = (k_ref[rows, :].astype(F32) * jnp.exp(btot - bcum)).astype(BF16)
        decay = jnp.exp(btot)
        q = (q_ref[rows, :].astype(F32) * (GLA_DK_HEAD ** -0.5)).astype(BF16)
        for h in range(GLA_HEADS):
            kcols = slice(h * GLA_DK_HEAD, (h + 1) * GLA_DK_HEAD)
            vcols = slice(h * GLA_DV_HEAD, (h + 1) * GLA_DV_HEAD)
            kv_t = lax.dot_general(v_ref[rows, vcols], k_dec[:, kcols],
                                   (((0,), (0,)), ((), ())), preferred_element_type=F32)
            st = st_ref[h] * decay[:, kcols] + kv_t
            st_ref[h] = st
            o = lax.dot_general(q[:, kcols], st.astype(BF16),
                                (((1,), (1,)), ((), ())), preferred_element_type=F32)
            o = o * lax.rsqrt(jnp.mean(o * o, axis=-1, keepdims=True) + EPS) * ng_ref[...]
            o = o * _silu(g_ref[rows, vcols].astype(F32))
            o_ref[rows, vcols] = o.astype(o_ref.dtype)
        return carry

    lax.fori_loop(0, GLA_ROWS // CHUNK, chunk, 0)


def _gla_scan(proj, log_a, norm_g):
    per_seq = SEQ // GLA_ROWS
    rows = lambda width, blk: pl.BlockSpec((GLA_ROWS, width), lambda b, s: (b * per_seq + s, blk))
    return pl.pallas_call(
        _gla_kernel,
        grid=(BATCH, per_seq),
        in_specs=[
            rows(GLA_DK, 0), rows(GLA_DK, 1), rows(GLA_DV, 1), rows(GLA_DV, 2),
            rows(GLA_DK, 0),
            pl.BlockSpec((1, GLA_DV_HEAD), lambda b, s: (0, 0)),
        ],
        out_specs=rows(GLA_DV, 0),
        out_shape=jax.ShapeDtypeStruct((TOKENS, GLA_DV), BF16),
        scratch_shapes=[pltpu.VMEM((GLA_HEADS, GLA_DV_HEAD, GLA_DK_HEAD), F32)],
        compiler_params=_params(2),
        name="gla_scan",
    )(proj, proj, proj, proj, log_a, norm_g.reshape(1, GLA_DV_HEAD))


ROUTE_TM = 512
META_E0, META_E1, META_G0, META_G1, META_R0, META_R1 = range(6)


def _route_kernel(x_ref, g_ref, sh_ref, sc_ref, r_ref, h_ref, meta_ref, cnt_ref, run_ref):
    @pl.when(pl.program_id(0) == 0)
    def _():
        run_ref[...] = jnp.zeros(run_ref.shape, F32)

    h = _norm_mod(x_ref[...], g_ref[...], sc_ref[...], sh_ref[...])
    h_ref[...] = h
    logits =jnp.dot(h, r_ref[...], preferred_element_type=F32, precision=lax.Precision.HIGHEST)
    lane = lax.broadcasted_iota(jnp.int32, logits.shape, 1)
    logits = jnp.where(lane < N_EXPERTS, logits, -jnp.inf)
    m0 = jnp.max(logits, axis=1, keepdims=True)
    e0 = jnp.min(jnp.where(logits == m0, lane, LANES), axis=1, keepdims=True)
    rest = jnp.where(lane == e0, -jnp.inf, logits)
    m1 = jnp.max(rest, axis=1, keepdims=True)
    e1 = jnp.min(jnp.where(rest == m1, lane, LANES), axis=1, keepdims=True)
    p = jnp.exp(m1 - m0)
    gate0 = 1.0 / (1.0 + p)
    gate1 = p / (1.0 + p)

    hot0 = (lane == e0).astype(F32)
    hot1 = (lane == e1).astype(F32)
    both = hot0 + hot1
    row = lax.broadcasted_iota(jnp.int32, (ROUTE_TM, ROUTE_TM), 0)
    col = lax.broadcasted_iota(jnp.int32, (ROUTE_TM, ROUTE_TM), 1)
    before = _dot((col < row).astype(BF16), both.astype(BF16)) + run_ref[0:1, :]
    rank0 = jnp.sum(hot0 * before, axis=1, keepdims=True)
    rank1 = jnp.sum(hot1 * before, axis=1, keepdims=True)
    run_ref[...] = run_ref[...] + jnp.sum(both, axis=0, keepdims=True)
    cnt_ref[...] = run_ref[...]

    meta = jnp.zeros(logits.shape, F32)
    for lane_id, val in ((META_E0, e0.astype(F32)), (META_E1, e1.astype(F32)), (META_G0, gate0),
                         (META_G1, gate1), (META_R0, rank0), (META_R1, rank1)):
        meta = jnp.where(lane == lane_id, val, meta)
    meta_ref[...] = meta


def _route(x, g, sh, sc, router):
    rows, vec, seq_vec = _row_specs(ROUTE_TM)
    router = jnp.pad(router, ((0, 0), (0, LANES - N_EXPERTS)))
    return pl.pallas_call(
        _route_kernel,
        grid=(TOKENS // ROUTE_TM,),
        in_specs=[rows, vec, seq_vec, seq_vec, pl.BlockSpec((D_MODEL, LANES), lambda i: (0, 0))],
        out_specs=[
            rows,
            pl.BlockSpec((ROUTE_TM, LANES), lambda i: (i, 0)),
            pl.BlockSpec((SUBLANES, LANES), lambda i: (0, 0)),
        ],
        out_shape=[
            jax.ShapeDtypeStruct((TOKENS, D_MODEL), F32),
            jax.ShapeDtypeStruct((TOKENS, LANES), F32),
            jax.ShapeDtypeStruct((SUBLANES, LANES), F32),
        ],
        scratch_shapes=[pltpu.VMEM((SUBLANES, LANES), F32)],
        compiler_params=_params(1),
        name="moe_route",
    )(x, g.reshape(1, D_MODEL), sh, sc, router)


def _row_copy(src_hbm, src_row, dst_groups, group, sub, sem):
    return pltpu.make_async_copy(
        src_hbm.at[pl.ds(src_row, 1), :], dst_groups.at[group, pl.ds(sub, 1), :], sem)


def _for_rows(n_rows, fn):
    def body(group, carry):
        for sub in range(SUBLANES):
            fn(group, sub)
        return carry

    lax.fori_loop(0, n_rows // SUBLANES, body, 0)


def _gather_kernel(tok_ref, h_hbm, o_ref, buf_ref, sem):
    b = pl.program_id(0)

    def issue(blk):
        slot = blk & 1
        _for_rows(MOE_ROWS, lambda group, sub: _row_copy(
            h_hbm, tok_ref[blk * MOE_ROWS + group * SUBLANES + sub], buf_ref.at[slot], group, sub,
            sem.at[slot]).start())

    @pl.when(b == 0)
    def _():
        issue(b)

    @pl.when(b + 1 < pl.num_programs(0))
    def _():
        issue(b + 1)

    slot = b & 1
    _for_rows(MOE_ROWS, lambda group, sub: _row_copy(
        h_hbm, 0, buf_ref.at[slot], group, sub, sem.at[slot]).wait())
    o_ref[...] = buf_ref[slot].reshape(MOE_ROWS, D_MODEL).astype(o_ref.dtype)


def _gather_slots(slot_tok, h):
    return pl.pallas_call(
        _gather_kernel,
        grid_spec=pltpu.PrefetchScalarGridSpec(
            num_scalar_prefetch=1,
            grid=(N_SLOT_BLOCKS,),
            in_specs=[pl.BlockSpec(memory_space=pl.ANY)],
            out_specs=pl.BlockSpec((MOE_ROWS, D_MODEL), lambda b, tok: (b, 0)),
            scratch_shapes=[pltpu.VMEM((2, MOE_ROWS // SUBLANES, SUBLANES, D_MODEL), F32),
                            pltpu.SemaphoreType.DMA((2,))],
        ),
        out_shape=jax.ShapeDtypeStruct((N_SLOTS, D_MODEL), BF16),
        compiler_params=_params(1),
        name="moe_gather",
    )(slot_tok, h)


BLOCK_DMA_PRIORITY = 1
WEIGHT_DMA_PRIORITY = 1
CHUNK_BLOCKS = 2


def _grouped_kernel(layer, col_offsets, compute, first_ref, cnt_ref, a_hbm, w_hbm, o_hbm,
                    wb_ref, w_buf, a_buf, o_buf, sem_w, sem_in, sem_out):
    f, e = pl.program_id(0), pl.program_id(1)
    n, first = cnt_ref[e], first_ref[e]
    col = pl.multiple_of(f * MM_TN, MM_TN)
    n_full = n // CHUNK_BLOCKS
    has_tail = n % CHUNK_BLOCKS == 1
    tail_blk = first + CHUNK_BLOCKS * n_full
    tail_slot = n_full & 1
    last_e = e == N_EXPERTS - 1
    is_first_step = jnp.logical_and(f == 0, e == 0)
    is_last_step = jnp.logical_and(last_e, f == pl.num_programs(0) - 1)
    e_next = jnp.where(last_e, 0, e + 1)
    f_next = jnp.where(last_e, f + 1, f)
    w_slot = (f * N_EXPERTS + e) & 1

    def weight_copies(tile, expert, slot):
        return [pltpu.make_async_copy(
            w_hbm.at[layer, expert, :, pl.ds(pl.multiple_of(tile * MM_TN + off, MM_TN), MM_TN)],
            w_buf.at[slot, k], sem_w.at[slot]) for k, off in enumerate(col_offsets)]

    def copy_in(blk, n_blocks, slot):
        rows = n_blocks * MOE_ROWS
        return pltpu.make_async_copy(
            a_hbm.at[pl.ds(blk * MOE_ROWS, rows), :], a_buf.at[slot, pl.ds(0, rows), :], sem_in.at[slot])

    def copy_out(blk, n_blocks, slot):
        rows = n_blocks * MOE_ROWS
        return pltpu.make_async_copy(
            o_buf.at[slot, pl.ds(0, rows), :],
            o_hbm.at[pl.ds(blk * MOE_ROWS, rows), pl.ds(col, MM_TN)], sem_out.at[slot])

    def start_first_chunk(expert):
        @pl.when(cnt_ref[expert] >= CHUNK_BLOCKS)
        def _():
            copy_in(first_ref[expert], CHUNK_BLOCKS, 0).start()

        @pl.when(cnt_ref[expert] == 1)
        def _():
            copy_in(first_ref[expert], 1, 0).start()

    @pl.when(is_first_step)
    def _():
        for cp in weight_copies(f, e, w_slot):
            cp.start()
        start_first_chunk(e)

    for cp in weight_copies(f, e, w_slot):
        cp.wait()
    for k in range(len(col_offsets)):
        wb_ref[k] = w_buf[w_slot, k].astype(BF16)

    @pl.when(jnp.logical_not(is_last_step))
    def _():
        for cp in weight_copies(f_next, e_next, 1 - w_slot):
            cp.start(priority=WEIGHT_DMA_PRIORITY)

    def full_chunk(c, carry):
        slot = c & 1
        blk = first + CHUNK_BLOCKS * c
        copy_in(blk, CHUNK_BLOCKS, slot).wait()

        @pl.when(c + 1 < n_full)
        def _():
            copy_in(blk + CHUNK_BLOCKS, CHUNK_BLOCKS, 1 - slot).start()

        @pl.when(jnp.logical_and(c + 1 == n_full, has_tail))
        def _():
            copy_in(blk + CHUNK_BLOCKS, 1, 1 - slot).start()

        @pl.when(c >= 2)
        def _():
            copy_out(blk - 2 * CHUNK_BLOCKS, CHUNK_BLOCKS, slot).wait()

        o_buf[slot] = compute(a_buf[slot], wb_ref).astype(o_buf.dtype)
        copy_out(blk, CHUNK_BLOCKS, slot).start()
        return carry

    lax.fori_loop(0, n_full, full_chunk, 0)

    @pl.when(has_tail)
    def _():
        copy_in(tail_blk, 1, tail_slot).wait()

        @pl.when(n_full >= 2)
        def _():
            copy_out(tail_blk - 2 * CHUNK_BLOCKS, CHUNK_BLOCKS, tail_slot).wait()

        o_buf[tail_slot, 0:MOE_ROWS, :] = compute(
            a_buf[tail_slot, 0:MOE_ROWS, :], wb_ref).astype(o_buf.dtype)
        copy_out(tail_blk, 1, tail_slot).start()

    @pl.when(jnp.logical_not(is_last_step))
    def _():
        start_first_chunk(e_next)

    @pl.when(jnp.logical_and(n_full >= 2, jnp.logical_not(has_tail)))
    def _():
        copy_out(tail_blk - 2 * CHUNK_BLOCKS, CHUNK_BLOCKS, tail_slot).wait()

    @pl.when(n_full >= 1)
    def _():
        copy_out(tail_blk - CHUNK_BLOCKS, CHUNK_BLOCKS, 1 - tail_slot).wait()

    @pl.when(has_tail)
    def _():
        copy_out(tail_blk, 1, tail_slot).wait()

    @pl.when(last_e)
    def _():
        o_buf[0, 0:MOE_ROWS, :] = jnp.zeros((MOE_ROWS, MM_TN), o_buf.dtype)

        def zero_block(blk, carry):
            copy_out(blk, 1, 0).start()
            copy_out(blk, 1, 0).wait()
            return carry

        lax.fori_loop(first + n, N_SLOT_BLOCKS, zero_block, 0)


def _grouped_matmul(name, compute, first_blk, n_blk, a, w4, layer, col_offsets, n_out, out_dtype):
    k = a.shape[1]
    n_w = len(col_offsets)
    return pl.pallas_call(
        functools.partial(_grouped_kernel, layer, col_offsets, compute),
        grid_spec=pltpu.PrefetchScalarGridSpec(
            num_scalar_prefetch=2,
            grid=(n_out // MM_TN, N_EXPERTS),
            in_specs=[pl.BlockSpec(memory_space=pl.ANY), pl.BlockSpec(memory_space=pl.ANY)],
            out_specs=pl.BlockSpec(memory_space=pl.ANY),
            scratch_shapes=[
                pltpu.VMEM((n_w, k, MM_TN), BF16),
                pltpu.VMEM((2, n_w, k, MM_TN), F32),
                pltpu.VMEM((2, CHUNK_BLOCKS * MOE_ROWS, k), BF16),
                pltpu.VMEM((2, CHUNK_BLOCKS * MOE_ROWS, MM_TN), out_dtype),
                pltpu.SemaphoreType.DMA((2,)),
                pltpu.SemaphoreType.DMA((2,)),
                pltpu.SemaphoreType.DMA((2,)),
            ],
        ),
        out_shape=jax.ShapeDtypeStruct((N_SLOTS, n_out), out_dtype),
        compiler_params=_params(2),
        name=name,
    )(first_blk, n_blk, a, w4)


def _swiglu_tile(a, wb_ref):
    return _silu(_dot(a, wb_ref[0])) * _dot(a, wb_ref[1])


def _down_tile(a, wb_ref):
    return _dot(a, wb_ref[0])


COMB_TM = 256


def _combine_kernel(final, dest_ref, x_ref, meta_ref, g2_ref, y_hbm, *refs):
    if final:
        fg_ref, o_ref, buf_ref, sem = refs
    else:
        ng_ref, nsh_ref, nsc_ref, o_ref, h_ref, buf_ref, sem = refs
    i = pl.program_id(0)

    def issue(tile):
        slot = tile & 1
        for k in range(TOP_K):
            _for_rows(COMB_TM, lambda group, sub: _row_copy(
                y_hbm, dest_ref[TOP_K * (tile * COMB_TM + group * SUBLANES + sub) + k],
                buf_ref.at[slot, k], group, sub, sem.at[slot]).start(priority=BLOCK_DMA_PRIORITY))

    @pl.when(i == 0)
    def _():
        issue(i)

    @pl.when(i + 1 < pl.num_programs(0))
    def _():
        issue(i + 1)

    slot = i & 1
    for k in range(TOP_K):
        _for_rows(COMB_TM, lambda group, sub: _row_copy(
            y_hbm, 0, buf_ref.at[slot, k], group, sub, sem.at[slot]).wait())
    meta = meta_ref[...]
    y0 = buf_ref[slot, 0].reshape(COMB_TM, D_MODEL)
    y1 = buf_ref[slot, 1].reshape(COMB_TM, D_MODEL)
    f = meta[:, META_G0:META_G0 + 1] * y0 + meta[:, META_G1:META_G1 + 1] * y1
    x_new = x_ref[...] + g2_ref[...] * f
    if final:
        ms = jnp.mean(x_new * x_new, axis=-1, keepdims=True)
        o_ref[...] = x_new * lax.rsqrt(ms + EPS) * fg_ref[...]
    else:
        o_ref[...] = x_new
        h_ref[...] = _norm_mod(x_new, ng_ref[...], nsc_ref[...], nsh_ref[...]).astype(h_ref.dtype)


def _combine(dest_flat, x, meta, g2, y_slots, final_g=None, next_norm=None):
    final = next_norm is None
    per_seq = SEQ // COMB_TM
    rows = pl.BlockSpec((COMB_TM, D_MODEL), lambda i, d: (i, 0))
    vec = pl.BlockSpec((1, D_MODEL), lambda i, d: (0, 0))
    seq_vec = pl.BlockSpec((None, 1, D_MODEL), lambda i, d: (i // per_seq, 0, 0))
    if final:
        extra_in, extra_specs = (final_g.reshape(1, D_MODEL),), [vec]
        out_specs, out_shape = rows, jax.ShapeDtypeStruct((TOKENS, D_MODEL), F32)
    else:
        g, sh, sc = next_norm
        extra_in, extra_specs = (g.reshape(1, D_MODEL), sh, sc), [vec, seq_vec, seq_vec]
        out_specs = [rows, rows]
        out_shape = [jax.ShapeDtypeStruct((TOKENS, D_MODEL), F32),
                     jax.ShapeDtypeStruct((TOKENS, D_MODEL), BF16)]
    return pl.pallas_call(
        functools.partial(_combine_kernel, final),
        grid_spec=pltpu.PrefetchScalarGridSpec(
            num_scalar_prefetch=1,
            grid=(TOKENS // COMB_TM,),
            in_specs=[rows, pl.BlockSpec((COMB_TM, LANES), lambda i, d: (i, 0)), seq_vec,
                      pl.BlockSpec(memory_space=pl.ANY)] + extra_specs,
            out_specs=out_specs,
            scratch_shapes=[pltpu.VMEM((2, TOP_K, COMB_TM // SUBLANES, SUBLANES, D_MODEL), F32),
                            pltpu.SemaphoreType.DMA((2,))],
        ),
        out_shape=out_shape,
        compiler_params=_params(1),
        name="moe_combine",
    )(dest_flat, x, meta, g2, y_slots, *extra_in)


def _slot_plan(meta, counts):
    top_e = meta[:, META_E0:META_E1 + 1].astype(jnp.int32)
    rank = meta[:, META_R0:META_R1 + 1].astype(jnp.int32)
    counts = counts[0, :N_EXPERTS].astype(jnp.int32)
    n_blk = (counts + MOE_ROWS - 1) // MOE_ROWS
    first_blk = jnp.cumsum(n_blk) - n_blk
    dest = (first_blk[top_e] * MOE_ROWS + rank).reshape(TOKENS * TOP_K)
    pair_tok = jnp.arange(TOKENS * TOP_K, dtype=jnp.int32) // TOP_K
    slot_tok = jnp.zeros((N_SLOTS,), jnp.int32).at[dest].set(pair_tok)
    return dest, slot_tok, first_blk.astype(jnp.int32), n_blk.astype(jnp.int32)


def kernel(x, c, ada_w, ada_b, norm_g, conv_w_in, conv_k, conv_w_out, gla_w_in, gla_w_gk, gla_b_gk,
           gla_norm_g, gla_w_out, ffn_w13, ffn_w2, moe_router, moe_w13, moe_w2, final_g):
    assert x.shape == (BATCH, SEQ, D_MODEL) and x.dtype == F32
    assert DEPTH % 2 == 0
    mod = _ada_all(c, ada_w, ada_b)[:, :BATCH]
    xt = x.reshape(TOKENS, D_MODEL)
    gla_w_in_t = jnp.swapaxes(gla_w_in, 1, 2)
    mods = [[mod[i, :, n * D_MODEL:(n + 1) * D_MODEL].reshape(BATCH, 1, D_MODEL) for n in range(6)]
            for i in range(DEPTH)]
    h = None
    for i in range(DEPTH):
        j = i // 2
        sh1, sc1, g1, sh2, sc2, g2 = mods[i]
        if h is None:
            h = _norm_modulate(xt, norm_g[i, 0], sh1, sc1)
        if i % 2 == 0:
            y = _conv_in(h, conv_w_in, conv_k, j)
            xt = _matmul_residual(y, conv_w_out, j, xt, g1)
            h = _norm_modulate(xt, norm_g[i, 1], sh2, sc2)
            t = _ffn_up(h, ffn_w13, j)
            xt = _matmul_residual(t, ffn_w2, j, xt, g2)
            h = None
        else:
            proj = _matmul_nt(h, gla_w_in_t, j, GLA_MAIN, BF16)
            log_a = _gla_log_decay(h, gla_w_in_t, j, gla_w_gk[j], gla_b_gk[j])
            o = _gla_scan(proj, log_a, gla_norm_g[j])
            xt = _matmul_residual(o, gla_w_out, j, xt, g1)
            h32, meta, counts = _route(xt, norm_g[i, 1], sh2, sc2, moe_router[j])
            dest, slot_tok, first_blk, n_blk = _slot_plan(meta, counts)
            xs = _gather_slots(slot_tok, h32)
            t = _grouped_matmul("moe_up", _swiglu_tile, first_blk, n_blk, xs, moe_w13, j,
                                (0, D_FF), D_FF, BF16)
            y = _grouped_matmul("moe_down", _down_tile, first_blk, n_blk, t, moe_w2, j,
                                (0,), D_MODEL, F32)
            if i == DEPTH - 1:
                xt = _combine(dest, xt, meta, g2, y, final_g=final_g)
            else:
                nsh, nsc = mods[i + 1][0], mods[i + 1][1]
                xt, h = _combine(dest, xt, meta, g2, y, next_norm=(norm_g[i + 1, 0], nsh, nsc))
    return xt.reshape(BATCH, SEQ, D_MODEL)
```

```python
import functools

import jax
import jax.numpy as jnp
from jax import lax
from jax.experimental import pallas as pl
from jax.experimental.pallas import tpu as pltpu

D_MODEL = 2048
BATCH = 4
SEQ = 2048
TOKENS = BATCH * SEQ
DEPTH = 4
CHUNK = 64
EPS = 1e-6
CONV_W = 3
GLA_HEADS = 4
GLA_DK = D_MODEL // 2
GLA_DV = D_MODEL
GLA_DK_HEAD = GLA_DK // GLA_HEADS
GLA_DV_HEAD = GLA_DV // GLA_HEADS
GLA_GATE_RANK = 16
GLA_GATE_NORM = 16.0
GLA_MAIN = 2 * GLA_DK + 2 * GLA_DV
D_FF = 5632
N_EXPERTS = 8
TOP_K = 2

LANES = 128
SUBLANES = 8
MXU_DIM = 256
VMEM_LIMIT = 56 * 1024 * 1024

MOE_ROWS = 256
N_SLOT_BLOCKS = TOKENS * TOP_K // MOE_ROWS + N_EXPERTS
N_SLOTS = N_SLOT_BLOCKS * MOE_ROWS

F32 = jnp.float32
BF16 = jnp.bfloat16


def _params(n_axes):
    return pltpu.CompilerParams(
        dimension_semantics=("arbitrary",) * n_axes, vmem_limit_bytes=VMEM_LIMIT)


def _dot(a, b):
    return jnp.dot(a, b, preferred_element_type=F32)


def _silu(v):
    return v * jax.nn.sigmoid(v)


ADA_TN = 1024


def _ada_kernel(c_ref, w_ref, b_ref, o_ref):
    c_act = _silu(c_ref[...]).astype(BF16)
    o_ref[...] = _dot(c_act, w_ref[...].astype(BF16)) + b_ref[...]


def _ada_all(c, ada_w, ada_b):
    c_pad = jnp.pad(c, ((0, SUBLANES - BATCH), (0, 0)))
    n = 6 * D_MODEL
    return pl.pallas_call(
        _ada_kernel,
        grid=(DEPTH, n // ADA_TN),
        in_specs=[
            pl.BlockSpec((SUBLANES, D_MODEL), lambda l, j: (0, 0)),
            pl.BlockSpec((None, D_MODEL, ADA_TN), lambda l, j: (l, 0, j)),
            pl.BlockSpec((None, 1, ADA_TN), lambda l, j: (l, 0, j)),
        ],
        out_specs=pl.BlockSpec((None, SUBLANES, ADA_TN), lambda l, j: (l, 0, j)),
        out_shape=jax.ShapeDtypeStruct((DEPTH, SUBLANES, n), F32),
        compiler_params=_params(2),
        name="ada_mod",
    )(c_pad, ada_w, ada_b.reshape(DEPTH, 1, n))


NORM_TM = 512


def _norm_mod(x, g, sc, sh):
    ms = jnp.mean(x * x, axis=-1, keepdims=True)
    y = x * lax.rsqrt(ms + EPS) * g
    return y * (1.0 + sc) + sh


def _norm_mod_kernel(x_ref, g_ref, sh_ref, sc_ref, h_ref):
    h_ref[...] = _norm_mod(x_ref[...], g_ref[...], sc_ref[...], sh_ref[...]).astype(h_ref.dtype)


def _row_specs(tm):
    per_seq = SEQ // tm
    rows = pl.BlockSpec((tm, D_MODEL), lambda i: (i, 0))
    vec = pl.BlockSpec((1, D_MODEL), lambda i: (0, 0))
    seq_vec = pl.BlockSpec((None, 1, D_MODEL), lambda i: (i // per_seq, 0, 0))
    return rows, vec, seq_vec


def _norm_modulate(x, g, sh, sc):
    rows, vec, seq_vec = _row_specs(NORM_TM)
    return pl.pallas_call(
        _norm_mod_kernel,
        grid=(TOKENS // NORM_TM,),
        in_specs=[rows, vec, seq_vec, seq_vec],
        out_specs=rows,
        out_shape=jax.ShapeDtypeStruct((TOKENS, D_MODEL), BF16),
        compiler_params=_params(1),
        name="norm_mod",
    )(x, g.reshape(1, D_MODEL), sh, sc)


MM_TM = 1024
MM_TN = 512


def _cast_weights_once(w_refs, wb_refs):
    @pl.when(pl.program_id(1) == 0)
    def _():
        for w_ref, wb_ref in zip(w_refs, wb_refs):
            wb_ref[...] = w_ref[...].astype(BF16)


def _dot_nt(a, b_t):
    return lax.dot_general(a, b_t, (((1,), (1,)), ((), ())), preferred_element_type=F32)


def _mm_nt_kernel(a_ref, wt_ref, o_ref, wb_ref):
    _cast_weights_once((wt_ref,), (wb_ref,))
    o_ref[...] = _dot_nt(a_ref[...], wb_ref[...]).astype(o_ref.dtype)


def _matmul_nt(a, wt3, layer, n_out, out_dtype):
    k = a.shape[1]
    tn = _wide_col_tile(k)
    return pl.pallas_call(
        _mm_nt_kernel,
        grid=(n_out // tn, TOKENS // MM_TM),
        in_specs=[
            pl.BlockSpec((MM_TM, k), lambda j, i: (i, 0)),
            pl.BlockSpec((None, tn, k), lambda j, i: (layer, j, 0)),
        ],
        out_specs=pl.BlockSpec((MM_TM, tn), lambda j, i: (i, j)),
        out_shape=jax.ShapeDtypeStruct((TOKENS, n_out), out_dtype),
        scratch_shapes=[pltpu.VMEM((tn, k), BF16)],
        compiler_params=_params(2),
        name="matmul_nt",
    )(a, wt3)


RES_ROW_BYTES = 6 * 1024 * 1024
WIDE_TILE_BYTES = 12 * 1024 * 1024


def _wide_col_tile(k):
    return min(2 * MM_TN, WIDE_TILE_BYTES // (4 * k) // MXU_DIM * MXU_DIM)


def _mm_res_kernel(a_ref, w_ref, x_ref, g_ref, o_ref, wb_ref):
    _cast_weights_once((w_ref,), (wb_ref,))
    o_ref[...] = x_ref[...] + g_ref[...] * _dot(a_ref[...], wb_ref[...])


def _matmul_residual(a, w3, layer, x, gate):
    k = a.shape[1]
    tm = min(MM_TM, RES_ROW_BYTES // (2 * k) // MXU_DIM * MXU_DIM)
    tn = _wide_col_tile(k)
    per_seq = SEQ // tm
    return pl.pallas_call(
        _mm_res_kernel,
        grid=(D_MODEL // tn, TOKENS // tm),
        in_specs=[
            pl.BlockSpec((tm, k), lambda j, i: (i, 0)),
            pl.BlockSpec((None, k, tn), lambda j, i: (layer, 0, j)),
            pl.BlockSpec((tm, tn), lambda j, i: (i, j)),
            pl.BlockSpec((None, 1, tn), lambda j, i: (i // per_seq, 0, j)),
        ],
        out_specs=pl.BlockSpec((tm, tn), lambda j, i: (i, j)),
        out_shape=jax.ShapeDtypeStruct((TOKENS, D_MODEL), F32),
        scratch_shapes=[pltpu.VMEM((k, tn), BF16)],
        compiler_params=_params(2),
        name="matmul_residual",
    )(a, w3, x, gate)


CONV_TM = 1024


def _conv_in_kernel(a_ref, wgb_ref, wgc_ref, wu_ref, kc_ref, o_ref, wb_ref, v_ref):
    i = pl.program_id(1)
    _cast_weights_once((wgb_ref, wgc_ref, wu_ref), (wb_ref.at[0], wb_ref.at[1], wb_ref.at[2]))

    @pl.when(i % (SEQ // CONV_TM) == 0)
    def _():
        v_ref[0:SUBLANES, :] = jnp.zeros((SUBLANES, MM_TN), F32)

    a = a_ref[...]
    gc = _dot(a, wb_ref[1])
    u = _dot(a, wb_ref[2])
    v_ref[SUBLANES:SUBLANES + CONV_TM, :] = gc * u
    conv = kc_ref[2:3, :] * v_ref[SUBLANES:SUBLANES + CONV_TM, :]
    conv = conv + kc_ref[1:2, :] * v_ref[SUBLANES - 1:SUBLANES - 1 + CONV_TM, :]
    conv = conv + kc_ref[0:1, :] * v_ref[SUBLANES - 2:SUBLANES - 2 + CONV_TM, :]
    gb = _dot(a, wb_ref[0])
    o_ref[...] = (gb * conv).astype(o_ref.dtype)
    v_ref[0:SUBLANES, :] = v_ref[CONV_TM:CONV_TM + SUBLANES, :]


def _conv_in(h, conv_w_in, conv_k, layer):
    nb = D_MODEL // MM_TN
    w_spec = lambda off: pl.BlockSpec((None, D_MODEL, MM_TN), lambda j, i: (layer, 0, j + off))
    return pl.pallas_call(
        _conv_in_kernel,
        grid=(nb, TOKENS // CONV_TM),
        in_specs=[
            pl.BlockSpec((CONV_TM, D_MODEL), lambda j, i: (i, 0)),
            w_spec(0), w_spec(nb), w_spec(2 * nb),
            pl.BlockSpec((None, CONV_W, MM_TN), lambda j, i: (layer, 0, j)),
        ],
        out_specs=pl.BlockSpec((CONV_TM, MM_TN), lambda j, i: (i, j)),
        out_shape=jax.ShapeDtypeStruct((TOKENS, D_MODEL), BF16),
        scratch_shapes=[
            pltpu.VMEM((3, D_MODEL, MM_TN), BF16),
            pltpu.VMEM((CONV_TM + SUBLANES, MM_TN), F32),
        ],
        compiler_params=_params(2),
        name="conv_in",
    )(h, conv_w_in, conv_w_in, conv_w_in, conv_k)


def _ffn_up_kernel(a_ref, w1_ref, w3_ref, o_ref, wb_ref):
    _cast_weights_once((w1_ref, w3_ref), (wb_ref.at[0], wb_ref.at[1]))
    a = a_ref[...]
    p = _dot(a, wb_ref[0])
    q = _dot(a, wb_ref[1])
    o_ref[...] = (_silu(p) * q).astype(o_ref.dtype)


def _ffn_up(h, w13, layer):
    nb = D_FF // MM_TN
    w_spec = lambda off: pl.BlockSpec((None, D_MODEL, MM_TN), lambda j, i: (layer, 0, j + off))
    return pl.pallas_call(
        _ffn_up_kernel,
        grid=(nb, TOKENS // MM_TM),
        in_specs=[pl.BlockSpec((MM_TM, D_MODEL), lambda j, i: (i, 0)), w_spec(0), w_spec(nb)],
        out_specs=pl.BlockSpec((MM_TM, MM_TN), lambda j, i: (i, j)),
        out_shape=jax.ShapeDtypeStruct((TOKENS, D_FF), BF16),
        scratch_shapes=[pltpu.VMEM((2, D_MODEL, MM_TN), BF16)],
        compiler_params=_params(2),
        name="ffn_up",
    )(h, w13, w13)


LOGA_TM = 512


def _loga_kernel(h_ref, wl_ref, wgk_ref, bgk_ref, o_ref):
    low = _dot_nt(h_ref[...], wl_ref[...].astype(BF16))
    z =_dot(low.astype(BF16), wgk_ref[...].astype(BF16)) + bgk_ref[...]
    log_sig = jnp.minimum(z, 0.0) - jnp.log1p(jnp.exp(-jnp.abs(z)))
    o_ref[...] = log_sig * (1.0 / GLA_GATE_NORM)


def _gla_log_decay(h, w_in_t, layer, w_gk, b_gk):
    return pl.pallas_call(
        _loga_kernel,
        grid=(TOKENS // LOGA_TM,),
        in_specs=[
            pl.BlockSpec((LOGA_TM, D_MODEL), lambda i: (i, 0)),
            pl.BlockSpec((None, GLA_GATE_RANK, D_MODEL), lambda i: (layer, GLA_MAIN // GLA_GATE_RANK, 0)),
            pl.BlockSpec((GLA_GATE_RANK, GLA_DK), lambda i: (0, 0)),
            pl.BlockSpec((1, GLA_DK), lambda i: (0, 0)),
        ],
        out_specs=pl.BlockSpec((LOGA_TM, GLA_DK), lambda i: (i, 0)),
        out_shape=jax.ShapeDtypeStruct((TOKENS, GLA_DK), F32),
        compiler_params=_params(1),
        name="gla_log_decay",
    )(h, w_in_t, w_gk, b_gk.reshape(1, GLA_DK))


GLA_ROWS = 512


def _gla_kernel(q_ref, k_ref, v_ref, g_ref, la_ref, ng_ref, o_ref, st_ref):
    @pl.when(pl.program_id(1) == 0)
    def _():
        st_ref[...] = jnp.zeros(st_ref.shape, F32)

    row = lax.broadcasted_iota(jnp.int32, (CHUNK, CHUNK), 0)
    col = lax.broadcasted_iota(jnp.int32, (CHUNK, CHUNK), 1)
    tri = (col <= row).astype(BF16)

    def chunk(ci, carry):
        rows = pl.ds(pl.multiple_of(ci * CHUNK, CHUNK), CHUNK)
        la = la_ref[rows, :]
        la_hi = la.astype(BF16)
        la_lo = (la - la_hi.astype(F32)).astype(BF16)
        bcum = _dot(tri, la_hi) + _dot(tri, la_lo)
        btot = bcum[CHUNK - 1:CHUNK, :]
        k_dec = (k_ref[rows, :].astype(F32) * jnp.exp(btot - bcum)).astype(BF16)
        decay = jnp.exp(btot)
        q = (q_ref[rows, :].astype(F32) * (GLA_DK_HEAD ** -0.5)).astype(BF16)
        for h in range(GLA_HEADS):
            kcols = slice(h * GLA_DK_HEAD, (h + 1) * GLA_DK_HEAD)
            vcols = slice(h * GLA_DV_HEAD, (h + 1) * GLA_DV_HEAD)
            kv_t = lax.dot_general(v_ref[rows, vcols], k_dec[:, kcols],
                                   (((0,), (0,)), ((), ())), preferred_element_type=F32)
            st = st_ref[h] * decay[:, kcols] + kv_t
            st_ref[h] = st
            o = lax.dot_general(q[:, kcols], st.astype(BF16),
                                (((1,), (1,)), ((), ())), preferred_element_type=F32)
            o = o * lax.rsqrt(jnp.mean(o * o, axis=-1, keepdims=True) + EPS) * ng_ref[...]
            o = o * _silu(g_ref[rows, vcols].astype(F32))
            o_ref[rows, vcols] = o.astype(o_ref.dtype)
        return carry

    lax.fori_loop(0, GLA_ROWS // CHUNK, chunk, 0)


def _gla_scan(proj, log_a, norm_g):
    per_seq = SEQ // GLA_ROWS
    rows = lambda width, blk: pl.BlockSpec((GLA_ROWS, width), lambda b, s: (b * per_seq + s, blk))
    return pl.pallas_call(
        _gla_kernel,
        grid=(BATCH, per_seq),
        in_specs=[
            rows(GLA_DK, 0), rows(GLA_DK, 1), rows(GLA_DV, 1), rows(GLA_DV, 2),
            rows(GLA_DK, 0),
            pl.BlockSpec((1, GLA_DV_HEAD), lambda b, s: (0, 0)),
        ],
        out_specs=rows(GLA_DV, 0),
        out_shape=jax.ShapeDtypeStruct((TOKENS, GLA_DV), BF16),
        scratch_shapes=[pltpu.VMEM((GLA_HEADS, GLA_DV_HEAD, GLA_DK_HEAD), F32)],
        compiler_params=_params(2),
        name="gla_scan",
    )(proj, proj, proj, proj, log_a, norm_g.reshape(1, GLA_DV_HEAD))


ROUTE_TM = 512
META_E0, META_E1, META_G0, META_G1, META_R0, META_R1 = range(6)


def _route_kernel(x_ref, g_ref, sh_ref, sc_ref, r_ref, h_ref, meta_ref, cnt_ref, run_ref):
    @pl.when(pl.program_id(0) == 0)
    def _():
        run_ref[...] = jnp.zeros(run_ref.shape, F32)

    h = _norm_mod(x_ref[...], g_ref[...], sc_ref[...], sh_ref[...])
    h_ref[...] = h
    h_hi = h.astype(BF16)
    h_lo = (h - h_hi.astype(F32)).astype(BF16)
    r = r_ref[...]
    r_hi = r.astype(BF16)
    r_lo = (r - r_hi.astype(F32)).astype(BF16)
    logits = _dot(h_hi, r_hi) + (_dot(h_hi, r_lo) + _dot(h_lo, r_hi))
    lane = lax.broadcasted_iota(jnp.int32, logits.shape, 1)
    logits = jnp.where(lane < N_EXPERTS, logits, -jnp.inf)
    m0 = jnp.max(logits, axis=1, keepdims=True)
    e0 = jnp.min(jnp.where(logits == m0, lane, LANES), axis=1, keepdims=True)
    rest = jnp.where(lane == e0, -jnp.inf, logits)
    m1 = jnp.max(rest, axis=1, keepdims=True)
    e1 = jnp.min(jnp.where(rest == m1, lane, LANES), axis=1, keepdims=True)
    p = jnp.exp(m1 - m0)
    gate0 = 1.0 / (1.0 + p)
    gate1 = p / (1.0 + p)

    hot0 = (lane == e0).astype(F32)
    hot1 = (lane == e1).astype(F32)
    both = hot0 + hot1
    row = lax.broadcasted_iota(jnp.int32, (ROUTE_TM, ROUTE_TM), 0)
    col = lax.broadcasted_iota(jnp.int32, (ROUTE_TM, ROUTE_TM), 1)
    before = _dot((col < row).astype(BF16), both.astype(BF16)) + run_ref[0:1, :]
    rank0 = jnp.sum(hot0 * before, axis=1, keepdims=True)
    rank1 = jnp.sum(hot1 * before, axis=1, keepdims=True)
    run_ref[...] = run_ref[...] + jnp.sum(both, axis=0, keepdims=True)
    cnt_ref[...] = run_ref[...]

    meta = jnp.zeros(logits.shape, F32)
    for lane_id, val in ((META_E0, e0.astype(F32)), (META_E1, e1.astype(F32)), (META_G0, gate0),
                         (META_G1, gate1), (META_R0, rank0), (META_R1, rank1)):
        meta = jnp.where(lane == lane_id, val, meta)
    meta_ref[...] = meta


def _route(x, g, sh, sc, router):
    rows, vec, seq_vec = _row_specs(ROUTE_TM)
    router = jnp.pad(router, ((0, 0), (0, LANES - N_EXPERTS)))
    return pl.pallas_call(
        _route_kernel,
        grid=(TOKENS // ROUTE_TM,),
        in_specs=[rows, vec, seq_vec, seq_vec, pl.BlockSpec((D_MODEL, LANES), lambda i: (0, 0))],
        out_specs=[
            rows,
            pl.BlockSpec((ROUTE_TM, LANES), lambda i: (i, 0)),
            pl.BlockSpec((SUBLANES, LANES), lambda i: (0, 0)),
        ],
        out_shape=[
            jax.ShapeDtypeStruct((TOKENS, D_MODEL), F32),
            jax.ShapeDtypeStruct((TOKENS, LANES), F32),
            jax.ShapeDtypeStruct((SUBLANES, LANES), F32),
        ],
        scratch_shapes=[pltpu.VMEM((SUBLANES, LANES), F32)],
        compiler_params=_params(1),
        name="moe_route",
    )(x, g.reshape(1, D_MODEL), sh, sc, router)


def _row_copy(src_hbm, src_row, dst_groups, group, sub, sem):
    return pltpu.make_async_copy(
        src_hbm.at[pl.ds(src_row, 1), :], dst_groups.at[group, pl.ds(sub, 1), :], sem)


def _for_rows(n_rows, fn):
    def body(group, carry):
        for sub in range(SUBLANES):
            fn(group, sub)
        return carry

    lax.fori_loop(0, n_rows // SUBLANES, body, 0)


GATHER_ROWS = 2 * MOE_ROWS


def _gather_kernel(tok_ref, h_hbm, o_ref, buf_ref, sem):
    b = pl.program_id(0)

    def issue(tile):
        slot = tile & 1
        _for_rows(GATHER_ROWS, lambda group, sub: _row_copy(
            h_hbm, tok_ref[tile * GATHER_ROWS + group * SUBLANES + sub], buf_ref.at[slot], group, sub,
            sem.at[slot]).start())

    @pl.when(b == 0)
    def _():
        issue(b)

    @pl.when(b + 1 < pl.num_programs(0))
    def _():
        issue(b + 1)

    slot = b & 1
    _for_rows(GATHER_ROWS, lambda group, sub: _row_copy(
        h_hbm, 0, buf_ref.at[slot], group, sub, sem.at[slot]).wait())
    o_ref[...] = buf_ref[slot].reshape(GATHER_ROWS, D_MODEL).astype(o_ref.dtype)


def _gather_slots(slot_tok, h):
    return pl.pallas_call(
        _gather_kernel,
        grid_spec=pltpu.PrefetchScalarGridSpec(
            num_scalar_prefetch=1,
            grid=(N_SLOTS // GATHER_ROWS,),
            in_specs=[pl.BlockSpec(memory_space=pl.ANY)],
            out_specs=pl.BlockSpec((GATHER_ROWS, D_MODEL), lambda b, tok: (b, 0)),
            scratch_shapes=[pltpu.VMEM((2, GATHER_ROWS // SUBLANES, SUBLANES, D_MODEL), F32),
                            pltpu.SemaphoreType.DMA((2,))],
        ),
        out_shape=jax.ShapeDtypeStruct((N_SLOTS, D_MODEL), BF16),
        compiler_params=_params(1),
        name="moe_gather",
    )(slot_tok, h)


BLOCK_DMA_PRIORITY = 1
WEIGHT_DMA_PRIORITY = 1
CHUNK_BLOCKS = 2


def _grouped_kernel(layer, col_offsets, compute, first_ref, cnt_ref, a_hbm, w_hbm, o_hbm,
                    wb_ref, w_buf, a_buf, o_buf, sem_w, sem_in, sem_out):
    f, e = pl.program_id(0), pl.program_id(1)
    n, first = cnt_ref[e], first_ref[e]
    col = pl.multiple_of(f * MM_TN, MM_TN)
    n_full = n // CHUNK_BLOCKS
    has_tail = n % CHUNK_BLOCKS == 1
    tail_blk = first + CHUNK_BLOCKS * n_full
    tail_slot = n_full & 1
    last_e = e == N_EXPERTS - 1
    is_first_step = jnp.logical_and(f == 0, e == 0)
    is_last_step = jnp.logical_and(last_e, f == pl.num_programs(0) - 1)
    e_next = jnp.where(last_e, 0, e + 1)
    f_next = jnp.where(last_e, f + 1, f)
    w_slot = (f * N_EXPERTS + e) & 1

    def weight_copies(tile, expert, slot):
        return [pltpu.make_async_copy(
            w_hbm.at[layer, expert, :, pl.ds(pl.multiple_of(tile * MM_TN + off, MM_TN), MM_TN)],
            w_buf.at[slot, k], sem_w.at[slot]) for k, off in enumerate(col_offsets)]

    def copy_in(blk, n_blocks, slot):
        rows = n_blocks * MOE_ROWS
        return pltpu.make_async_copy(
            a_hbm.at[pl.ds(blk * MOE_ROWS, rows), :], a_buf.at[slot, pl.ds(0, rows), :], sem_in.at[slot])

    def copy_out(blk, n_blocks, slot):
        rows = n_blocks * MOE_ROWS
        return pltpu.make_async_copy(
            o_buf.at[slot, pl.ds(0, rows), :],
            o_hbm.at[pl.ds(blk * MOE_ROWS, rows), pl.ds(col, MM_TN)], sem_out.at[slot])

    def start_first_chunk(expert):
        @pl.when(cnt_ref[expert] >= CHUNK_BLOCKS)
        def _():
            copy_in(first_ref[expert], CHUNK_BLOCKS, 0).start()

        @pl.when(cnt_ref[expert] == 1)
        def _():
            copy_in(first_ref[expert], 1, 0).start()

    @pl.when(is_first_step)
    def _():
        for cp in weight_copies(f, e, w_slot):
            cp.start()
        start_first_chunk(e)

    for cp in weight_copies(f, e, w_slot):
        cp.wait()
    for k in range(len(col_offsets)):
        wb_ref[k] = w_buf[w_slot, k].astype(BF16)

    @pl.when(jnp.logical_not(is_last_step))
    def _():
        for cp in weight_copies(f_next, e_next, 1 - w_slot):
            cp.start(priority=WEIGHT_DMA_PRIORITY)

    def full_chunk(c, carry):
        slot = c & 1
        blk = first + CHUNK_BLOCKS * c
        copy_in(blk, CHUNK_BLOCKS, slot).wait()

        @pl.when(c + 1 < n_full)
        def _():
            copy_in(blk + CHUNK_BLOCKS, CHUNK_BLOCKS, 1 - slot).start()

        @pl.when(jnp.logical_and(c + 1 == n_full, has_tail))
        def _():
            copy_in(blk + CHUNK_BLOCKS, 1, 1 - slot).start()

        @pl.when(c >= 2)
        def _():
            copy_out(blk - 2 * CHUNK_BLOCKS, CHUNK_BLOCKS, slot).wait()

        o_buf[slot] = compute(a_buf[slot], wb_ref).astype(o_buf.dtype)
        copy_out(blk, CHUNK_BLOCKS, slot).start()
        return carry

    lax.fori_loop(0, n_full, full_chunk, 0)

    @pl.when(has_tail)
    def _():
        copy_in(tail_blk, 1, tail_slot).wait()

        @pl.when(n_full >= 2)
        def _():
            copy_out(tail_blk - 2 * CHUNK_BLOCKS, CHUNK_BLOCKS, tail_slot).wait()

        o_buf[tail_slot, 0:MOE_ROWS, :] = compute(
            a_buf[tail_slot, 0:MOE_ROWS, :], wb_ref).astype(o_buf.dtype)
        copy_out(tail_blk, 1, tail_slot).start()

    @pl.when(jnp.logical_not(is_last_step))
    def _():
        start_first_chunk(e_next)

    @pl.when(jnp.logical_and(n_full >= 2, jnp.logical_not(has_tail)))
    def _():
        copy_out(tail_blk - 2 * CHUNK_BLOCKS, CHUNK_BLOCKS, tail_slot).wait()

    @pl.when(n_full >= 1)
    def _():
        copy_out(tail_blk - CHUNK_BLOCKS, CHUNK_BLOCKS, 1 - tail_slot).wait()

    @pl.when(has_tail)
    def _():
        copy_out(tail_blk, 1, tail_slot).wait()

    @pl.when(last_e)
    def _():
        o_buf[0, 0:MOE_ROWS, :] = jnp.zeros((MOE_ROWS, MM_TN), o_buf.dtype)

        def zero_block(blk, carry):
            copy_out(blk, 1, 0).start()
            copy_out(blk, 1, 0).wait()
            return carry

        lax.fori_loop(first + n, N_SLOT_BLOCKS, zero_block, 0)


def _grouped_matmul(name, compute, first_blk, n_blk, a, w4, layer, col_offsets, n_out, out_dtype):
    k = a.shape[1]
    n_w = len(col_offsets)
    return pl.pallas_call(
        functools.partial(_grouped_kernel, layer, col_offsets, compute),
        grid_spec=pltpu.PrefetchScalarGridSpec(
            num_scalar_prefetch=2,
            grid=(n_out // MM_TN, N_EXPERTS),
            in_specs=[pl.BlockSpec(memory_space=pl.ANY), pl.BlockSpec(memory_space=pl.ANY)],
            out_specs=pl.BlockSpec(memory_space=pl.ANY),
            scratch_shapes=[
                pltpu.VMEM((n_w, k, MM_TN), BF16),
                pltpu.VMEM((2, n_w, k, MM_TN), F32),
                pltpu.VMEM((2, CHUNK_BLOCKS * MOE_ROWS, k), BF16),
                pltpu.VMEM((2, CHUNK_BLOCKS * MOE_ROWS, MM_TN), out_dtype),
                pltpu.SemaphoreType.DMA((2,)),
                pltpu.SemaphoreType.DMA((2,)),
                pltpu.SemaphoreType.DMA((2,)),
            ],
        ),
        out_shape=jax.ShapeDtypeStruct((N_SLOTS, n_out), out_dtype),
        compiler_params=_params(2),
        name=name,
    )(first_blk, n_blk, a, w4)


def _swiglu_tile(a, wb_ref):
    return _silu(_dot(a, wb_ref[0])) * _dot(a, wb_ref[1])


def _down_tile(a, wb_ref):
    return _dot(a, wb_ref[0])


COMB_TM = 256


def _combine_kernel(final, dest_ref, x_ref, meta_ref, g2_ref, y_hbm, *refs):
    if final:
        fg_ref, o_ref, buf_ref, sem = refs
    else:
        ng_ref, nsh_ref, nsc_ref, o_ref, h_ref, buf_ref, sem = refs
    i = pl.program_id(0)

    def issue(tile):
        slot = tile & 1
        for k in range(TOP_K):
            _for_rows(COMB_TM, lambda group, sub: _row_copy(
                y_hbm, dest_ref[TOP_K * (tile * COMB_TM + group * SUBLANES + sub) + k],
                buf_ref.at[slot, k], group, sub, sem.at[slot]).start(priority=BLOCK_DMA_PRIORITY))

    @pl.when(i == 0)
    def _():
        issue(i)

    @pl.when(i + 1 < pl.num_programs(0))
    def _():
        issue(i + 1)

    slot = i & 1
    for k in range(TOP_K):
        _for_rows(COMB_TM, lambda group, sub: _row_copy(
            y_hbm, 0, buf_ref.at[slot, k], group, sub, sem.at[slot]).wait())
    meta = meta_ref[...]
    y0 = buf_ref[slot, 0].reshape(COMB_TM, D_MODEL)
    y1 = buf_ref[slot, 1].reshape(COMB_TM, D_MODEL)
    f = meta[:, META_G0:META_G0 + 1] * y0 + meta[:, META_G1:META_G1 + 1] * y1
    x_new = x_ref[...] + g2_ref[...] * f
    if final:
        ms = jnp.mean(x_new * x_new, axis=-1, keepdims=True)
        o_ref[...] = x_new * lax.rsqrt(ms + EPS) * fg_ref[...]
    else:
        o_ref[...] = x_new
        h_ref[...] = _norm_mod(x_new, ng_ref[...], nsc_ref[...], nsh_ref[...]).astype(h_ref.dtype)


def _combine(dest_flat, x, meta, g2, y_slots, final_g=None, next_norm=None):
    final = next_norm is None
    per_seq = SEQ // COMB_TM
    rows = pl.BlockSpec((COMB_TM, D_MODEL), lambda i, d: (i, 0))
    vec = pl.BlockSpec((1, D_MODEL), lambda i, d: (0, 0))
    seq_vec = pl.BlockSpec((None, 1, D_MODEL), lambda i, d: (i // per_seq, 0, 0))
    if final:
        extra_in, extra_specs = (final_g.reshape(1, D_MODEL),), [vec]
        out_specs, out_shape = rows, jax.ShapeDtypeStruct((TOKENS, D_MODEL), F32)
    else:
        g, sh, sc = next_norm
        extra_in, extra_specs = (g.reshape(1, D_MODEL), sh, sc), [vec, seq_vec, seq_vec]
        out_specs = [rows, rows]
        out_shape = [jax.ShapeDtypeStruct((TOKENS, D_MODEL), F32),
                     jax.ShapeDtypeStruct((TOKENS, D_MODEL), BF16)]
    return pl.pallas_call(
        functools.partial(_combine_kernel, final),
        grid_spec=pltpu.PrefetchScalarGridSpec(
            num_scalar_prefetch=1,
            grid=(TOKENS // COMB_TM,),
            in_specs=[rows, pl.BlockSpec((COMB_TM, LANES), lambda i, d: (i, 0)), seq_vec,
                      pl.BlockSpec(memory_space=pl.ANY)] + extra_specs,
            out_specs=out_specs,
            scratch_shapes=[pltpu.VMEM((2, TOP_K, COMB_TM // SUBLANES, SUBLANES, D_MODEL), F32),
                            pltpu.SemaphoreType.DMA((2,))],
        ),
        out_shape=out_shape,
        compiler_params=_params(1),
        name="moe_combine",
    )(dest_flat, x, meta, g2, y_slots, *extra_in)


def _slot_plan(meta, counts):
    top_e = meta[:, META_E0:META_E1 + 1].astype(jnp.int32)
    rank = meta[:, META_R0:META_R1 + 1].astype(jnp.int32)
    counts = counts[0, :N_EXPERTS].astype(jnp.int32)
    n_blk = (counts + MOE_ROWS - 1) // MOE_ROWS
    first_blk = jnp.cumsum(n_blk) - n_blk
    dest = (first_blk[top_e] * MOE_ROWS + rank).reshape(TOKENS * TOP_K)
    pair_tok = jnp.arange(TOKENS * TOP_K, dtype=jnp.int32) // TOP_K
    slot_tok = jnp.zeros((N_SLOTS,), jnp.int32).at[dest].set(pair_tok)
    return dest, slot_tok, first_blk.astype(jnp.int32), n_blk.astype(jnp.int32)


def kernel(x, c, ada_w, ada_b, norm_g, conv_w_in, conv_k, conv_w_out, gla_w_in, gla_w_gk, gla_b_gk,
           gla_norm_g, gla_w_out, ffn_w13, ffn_w2, moe_router, moe_w13, moe_w2, final_g):
    assert x.shape == (BATCH, SEQ, D_MODEL) and x.dtype == F32
    assert DEPTH % 2 == 0
    mod = _ada_all(c, ada_w, ada_b)[:, :BATCH]
    xt = x.reshape(TOKENS, D_MODEL)
    gla_w_in_t = jnp.swapaxes(gla_w_in, 1, 2)
    mods = [[mod[i, :, n * D_MODEL:(n + 1) * D_MODEL].reshape(BATCH, 1, D_MODEL) for n in range(6)]
            for i in range(DEPTH)]
    h = None
    for i in range(DEPTH):
        j = i // 2
        sh1, sc1, g1, sh2, sc2, g2 = mods[i]
        if h is None:
            h = _norm_modulate(xt, norm_g[i, 0], sh1, sc1)
        if i % 2 == 0:
            y = _conv_in(h, conv_w_in, conv_k, j)
            xt = _matmul_residual(y, conv_w_out, j, xt, g1)
            h = _norm_modulate(xt, norm_g[i, 1], sh2, sc2)
            t = _ffn_up(h, ffn_w13, j)
            xt = _matmul_residual(t, ffn_w2, j, xt, g2)
            h = None
        else:
            proj = _matmul_nt(h, gla_w_in_t, j, GLA_MAIN, BF16)
            log_a = _gla_log_decay(h, gla_w_in_t, j, gla_w_gk[j], gla_b_gk[j])
            o = _gla_scan(proj, log_a, gla_norm_g[j])
            xt = _matmul_residual(o, gla_w_out, j, xt, g1)
            h32, meta, counts = _route(xt, norm_g[i, 1], sh2, sc2, moe_router[j])
            dest, slot_tok, first_blk, n_blk = _slot_plan(meta, counts)
            xs = _gather_slots(slot_tok, h32)
            t = _grouped_matmul("moe_up", _swiglu_tile, first_blk, n_blk, xs, moe_w13, j,
                                (0, D_FF), D_FF, BF16)
            y = _grouped_matmul("moe_down", _down_tile, first_blk, n_blk, t, moe_w2, j,
                                (0,), D_MODEL, F32)
            if i == DEPTH - 1:
                xt = _combine(dest, xt, meta, g2, y, final_g=final_g)
            else:
                nsh, nsc = mods[i + 1][0], mods[i + 1][1]
                xt, h = _combine(dest, xt, meta, g2, y, next_norm=(norm_g[i + 1, 0], nsh, nsc))
    return xt.reshape(BATCH, SEQ, D_MODEL)
```

```python
import functools

import jax
import jax.numpy as jnp
from jax import lax
from jax.experimental import pallas as pl
from jax.experimental.pallas import tpu as pltpu

D_MODEL = 2048
BATCH = 4
SEQ = 2048
TOKENS = BATCH * SEQ
DEPTH = 4
CHUNK = 64
EPS = 1e-6
CONV_W = 3
GLA_HEADS = 4
GLA_DK = D_MODEL // 2
GLA_DV = D_MODEL
GLA_DK_HEAD = GLA_DK // GLA_HEADS
GLA_DV_HEAD = GLA_DV // GLA_HEADS
GLA_GATE_RANK = 16
GLA_GATE_NORM = 16.0
GLA_MAIN = 2 * GLA_DK + 2 * GLA_DV
D_FF = 5632
N_EXPERTS = 8
TOP_K = 2

LANES = 128
SUBLANES = 8
MXU_DIM = 256
VMEM_LIMIT = 56 * 1024 * 1024

MOE_ROWS = 256
N_SLOT_BLOCKS = TOKENS * TOP_K // MOE_ROWS + N_EXPERTS
N_SLOTS = N_SLOT_BLOCKS * MOE_ROWS

F32 = jnp.float32
BF16 = jnp.bfloat16


def _params(n_axes):
    return pltpu.CompilerParams(
        dimension_semantics=("arbitrary",) * n_axes, vmem_limit_bytes=VMEM_LIMIT)


def _dot(a, b):
    return jnp.dot(a, b, preferred_element_type=F32)


def _silu(v):
    return v * jax.nn.sigmoid(v)


ADA_TN = 1024


def _ada_kernel(c_ref, w_ref, b_ref, o_ref):
    c_act = _silu(c_ref[...]).astype(BF16)
    o_ref[...] = _dot(c_act, w_ref[...].astype(BF16)) + b_ref[...]


def _ada_all(c, ada_w, ada_b):
    c_pad = jnp.pad(c, ((0, SUBLANES - BATCH), (0, 0)))
    n = 6 * D_MODEL
    return pl.pallas_call(
        _ada_kernel,
        grid=(DEPTH, n // ADA_TN),
        in_specs=[
            pl.BlockSpec((SUBLANES, D_MODEL), lambda l, j: (0, 0)),
            pl.BlockSpec((None, D_MODEL, ADA_TN), lambda l, j: (l, 0, j)),
            pl.BlockSpec((None, 1, ADA_TN), lambda l, j: (l, 0, j)),
        ],
        out_specs=pl.BlockSpec((None, SUBLANES, ADA_TN), lambda l, j: (l, 0, j)),
        out_shape=jax.ShapeDtypeStruct((DEPTH, SUBLANES, n), F32),
        compiler_params=_params(2),
        name="ada_mod",
    )(c_pad, ada_w, ada_b.reshape(DEPTH, 1, n))


NORM_TM = 512


def _norm_mod(x, g, sc, sh):
    ms = jnp.mean(x * x, axis=-1, keepdims=True)
    y = x * lax.rsqrt(ms + EPS) * g
    return y * (1.0 + sc) + sh


def _norm_mod_kernel(x_ref, g_ref, sh_ref, sc_ref, h_ref):
    h_ref[...] = _norm_mod(x_ref[...], g_ref[...], sc_ref[...], sh_ref[...]).astype(h_ref.dtype)


def _row_specs(tm):
    per_seq = SEQ // tm
    rows = pl.BlockSpec((tm, D_MODEL), lambda i: (i, 0))
    vec = pl.BlockSpec((1, D_MODEL), lambda i: (0, 0))
    seq_vec = pl.BlockSpec((None, 1, D_MODEL), lambda i: (i // per_seq, 0, 0))
    return rows, vec, seq_vec


def _norm_modulate(x, g, sh, sc):
    rows, vec, seq_vec = _row_specs(NORM_TM)
    return pl.pallas_call(
        _norm_mod_kernel,
        grid=(TOKENS // NORM_TM,),
        in_specs=[rows, vec, seq_vec, seq_vec],
        out_specs=rows,
        out_shape=jax.ShapeDtypeStruct((TOKENS, D_MODEL), BF16),
        compiler_params=_params(1),
        name="norm_mod",
    )(x, g.reshape(1, D_MODEL), sh, sc)


MM_TM = 1024
MM_TN = 512


def _cast_weights_once(w_refs, wb_refs):
    @pl.when(pl.program_id(1) == 0)
    def _():
        for w_ref, wb_ref in zip(w_refs, wb_refs):
            wb_ref[...] = w_ref[...].astype(BF16)


def _dot_nt(a, b_t):
    return lax.dot_general(a, b_t, (((1,), (1,)), ((), ())), preferred_element_type=F32)


def _mm_nt_kernel(a_ref, wt_ref, o_ref, wb_ref):
    _cast_weights_once((wt_ref,), (wb_ref,))
    o_ref[...] = _dot_nt(a_ref[...], wb_ref[...]).astype(o_ref.dtype)


def _matmul_nt(a, wt3, layer, n_out, out_dtype):
    k = a.shape[1]
    tn = _wide_col_tile(k)
    return pl.pallas_call(
        _mm_nt_kernel,
        grid=(n_out // tn, TOKENS // MM_TM),
        in_specs=[
            pl.BlockSpec((MM_TM, k), lambda j, i: (i, 0)),
            pl.BlockSpec((None, tn, k), lambda j, i: (layer, j, 0)),
        ],
        out_specs=pl.BlockSpec((MM_TM, tn), lambda j, i: (i, j)),
        out_shape=jax.ShapeDtypeStruct((TOKENS, n_out), out_dtype),
        scratch_shapes=[pltpu.VMEM((tn, k), BF16)],
        compiler_params=_params(2),
        name="matmul_nt",
    )(a, wt3)


RES_ROW_BYTES = 6 * 1024 * 1024
WIDE_TILE_BYTES = 12 * 1024 * 1024


def _wide_col_tile(k):
    return min(2 * MM_TN, WIDE_TILE_BYTES // (4 * k) // MXU_DIM * MXU_DIM)


def _mm_res_kernel(a_ref, w_ref, x_ref, g_ref, o_ref, wb_ref):
    _cast_weights_once((w_ref,), (wb_ref,))
    o_ref[...] = x_ref[...] + g_ref[...] * _dot(a_ref[...], wb_ref[...])


def _matmul_residual(a, w3, layer, x, gate):
    k = a.shape[1]
    tm = min(MM_TM, RES_ROW_BYTES // (2 * k) // MXU_DIM * MXU_DIM)
    tn = _wide_col_tile(k)
    per_seq = SEQ // tm
    return pl.pallas_call(
        _mm_res_kernel,
        grid=(D_MODEL // tn, TOKENS // tm),
        in_specs=[
            pl.BlockSpec((tm, k), lambda j, i: (i, 0)),
            pl.BlockSpec((None, k, tn), lambda j, i: (layer, 0, j)),
            pl.BlockSpec((tm, tn), lambda j, i: (i, j)),
            pl.BlockSpec((None, 1, tn), lambda j, i: (i // per_seq, 0, j)),
        ],
        out_specs=pl.BlockSpec((tm, tn), lambda j, i: (i, j)),
        out_shape=jax.ShapeDtypeStruct((TOKENS, D_MODEL), F32),
        scratch_shapes=[pltpu.VMEM((k, tn), BF16)],
        compiler_params=_params(2),
        name="matmul_residual",
    )(a, w3, x, gate)


CONV_TM = 1024


def _conv_in_kernel(a_ref, wgb_ref, wgc_ref, wu_ref, kc_ref, o_ref, wb_ref, v_ref):
    i = pl.program_id(1)
    _cast_weights_once((wgb_ref, wgc_ref, wu_ref), (wb_ref.at[0], wb_ref.at[1], wb_ref.at[2]))

    @pl.when(i % (SEQ // CONV_TM) == 0)
    def _():
        v_ref[0:SUBLANES, :] = jnp.zeros((SUBLANES, MM_TN), F32)

    a = a_ref[...]
    gc = _dot(a, wb_ref[1])
    u = _dot(a, wb_ref[2])
    v_ref[SUBLANES:SUBLANES + CONV_TM, :] = gc * u
    conv = kc_ref[2:3, :] * v_ref[SUBLANES:SUBLANES + CONV_TM, :]
    conv = conv + kc_ref[1:2, :] * v_ref[SUBLANES - 1:SUBLANES - 1 + CONV_TM, :]
    conv = conv + kc_ref[0:1, :] * v_ref[SUBLANES - 2:SUBLANES - 2 + CONV_TM, :]
    gb = _dot(a, wb_ref[0])
    o_ref[...] = (gb * conv).astype(o_ref.dtype)
    v_ref[0:SUBLANES, :] = v_ref[CONV_TM:CONV_TM + SUBLANES, :]


def _conv_in(h, conv_w_in, conv_k, layer):
    nb = D_MODEL // MM_TN
    w_spec = lambda off: pl.BlockSpec((None, D_MODEL, MM_TN), lambda j, i: (layer, 0, j + off))
    return pl.pallas_call(
        _conv_in_kernel,
        grid=(nb, TOKENS // CONV_TM),
        in_specs=[
            pl.BlockSpec((CONV_TM, D_MODEL), lambda j, i: (i, 0)),
            w_spec(0), w_spec(nb), w_spec(2 * nb),
            pl.BlockSpec((None, CONV_W, MM_TN), lambda j, i: (layer, 0, j)),
        ],
        out_specs=pl.BlockSpec((CONV_TM, MM_TN), lambda j, i: (i, j)),
        out_shape=jax.ShapeDtypeStruct((TOKENS, D_MODEL), BF16),
        scratch_shapes=[
            pltpu.VMEM((3, D_MODEL, MM_TN), BF16),
            pltpu.VMEM((CONV_TM + SUBLANES, MM_TN), F32),
        ],
        compiler_params=_params(2),
        name="conv_in",
    )(h, conv_w_in, conv_w_in, conv_w_in, conv_k)


def _ffn_up_kernel(a_ref, w1_ref, w3_ref, o_ref, wb_ref):
    _cast_weights_once((w1_ref, w3_ref), (wb_ref.at[0], wb_ref.at[1]))
    a = a_ref[...]
    p = _dot(a, wb_ref[0])
    q = _dot(a, wb_ref[1])
    o_ref[...] = (_silu(p) * q).astype(o_ref.dtype)


def _ffn_up(h, w13, layer):
    nb = D_FF // MM_TN
    w_spec = lambda off: pl.BlockSpec((None, D_MODEL, MM_TN), lambda j, i: (layer, 0, j + off))
    return pl.pallas_call(
        _ffn_up_kernel,
        grid=(nb, TOKENS // MM_TM),
        in_specs=[pl.BlockSpec((MM_TM, D_MODEL), lambda j, i: (i, 0)), w_spec(0), w_spec(nb)],
        out_specs=pl.BlockSpec((MM_TM, MM_TN), lambda j, i: (i, j)),
        out_shape=jax.ShapeDtypeStruct((TOKENS, D_FF), BF16),
        scratch_shapes=[pltpu.VMEM((2, D_MODEL, MM_TN), BF16)],
        compiler_params=_params(2),
        name="ffn_up",
    )(h, w13, w13)


LOGA_TM = 512


def _loga_kernel(h_ref, wl_ref, wgk_ref, bgk_ref, o_ref):
    low = _dot_nt(h_ref[...], wl_ref[...].astype(BF16))
    z =_dot(low.astype(BF16), wgk_ref[...].astype(BF16)) + bgk_ref[...]
    log_sig = jnp.minimum(z, 0.0) - jnp.log1p(jnp.exp(-jnp.abs(z)))
    o_ref[...] = log_sig * (1.0 / GLA_GATE_NORM)


def _gla_log_decay(h, w_in_t, layer, w_gk, b_gk):
    return pl.pallas_call(
        _loga_kernel,
        grid=(TOKENS // LOGA_TM,),
        in_specs=[
            pl.BlockSpec((LOGA_TM, D_MODEL), lambda i: (i, 0)),
            pl.BlockSpec((None, GLA_GATE_RANK, D_MODEL), lambda i: (layer, GLA_MAIN // GLA_GATE_RANK, 0)),
            pl.BlockSpec((GLA_GATE_RANK, GLA_DK), lambda i: (0, 0)),
            pl.BlockSpec((1, GLA_DK), lambda i: (0, 0)),
        ],
        out_specs=pl.BlockSpec((LOGA_TM, GLA_DK), lambda i: (i, 0)),
        out_shape=jax.ShapeDtypeStruct((TOKENS, GLA_DK), F32),
        compiler_params=_params(1),
        name="gla_log_decay",
    )(h, w_in_t, w_gk, b_gk.reshape(1, GLA_DK))


GLA_ROWS = 512


def _gla_kernel(q_ref, k_ref, v_ref, g_ref, la_ref, ng_ref, o_ref, st_ref):
    @pl.when(pl.program_id(1) == 0)
    def _():
        st_ref[...] = jnp.zeros(st_ref.shape, F32)

    row = lax.broadcasted_iota(jnp.int32, (CHUNK, CHUNK), 0)
    col = lax.broadcasted_iota(jnp.int32, (CHUNK, CHUNK), 1)
    tri = (col <= row).astype(BF16)

    def chunk(ci, carry):
        rows = pl.ds(pl.multiple_of(ci * CHUNK, CHUNK), CHUNK)
        la = la_ref[rows, :]
        la_hi = la.astype(BF16)
        la_lo = (la - la_hi.astype(F32)).astype(BF16)
        bcum = _dot(tri, la_hi) + _dot(tri, la_lo)
        btot = bcum[CHUNK - 1:CHUNK, :]
        k_dec = (k_ref[rows, :].astype(F32) * jnp.exp(btot - bcum)).astype(BF16)
        decay = jnp.exp(btot)
        q = (q_ref[rows, :].astype(F32) * (GLA_DK_HEAD ** -0.5)).astype(BF16)
        for h in range(GLA_HEADS):
            kcols = slice(h * GLA_DK_HEAD, (h + 1) * GLA_DK_HEAD)
            vcols = slice(h * GLA_DV_HEAD, (h + 1) * GLA_DV_HEAD)
            kv_t = lax.dot_general(v_ref[rows, vcols], k_dec[:, kcols],
                                   (((0,), (0,)), ((), ())), preferred_element_type=F32)
            st = st_ref[h] * decay[:, kcols] + kv_t
            st_ref[h] = st
            o = lax.dot_general(q[:, kcols], st.astype(BF16),
                                (((1,), (1,)), ((), ())), preferred_element_type=F32)
            o = o * lax.rsqrt(jnp.mean(o * o, axis=-1, keepdims=True) + EPS) * ng_ref[...]
            o = o * _silu(g_ref[rows, vcols].astype(F32))
            o_ref[rows, vcols] = o.astype(o_ref.dtype)
        return carry

    lax.fori_loop(0, GLA_ROWS // CHUNK, chunk, 0)


def _gla_scan(proj, log_a, norm_g):
    per_seq = SEQ // GLA_ROWS
    rows = lambda width, blk: pl.BlockSpec((GLA_ROWS, width), lambda b, s: (b * per_seq + s, blk))
    return pl.pallas_call(
        _gla_kernel,
        grid=(BATCH, per_seq),
        in_specs=[
            rows(GLA_DK, 0), rows(GLA_DK, 1), rows(GLA_DV, 1), rows(GLA_DV, 2),
            rows(GLA_DK, 0),
            pl.BlockSpec((1, GLA_DV_HEAD), lambda b, s: (0, 0)),
        ],
        out_specs=rows(GLA_DV, 0),
        out_shape=jax.ShapeDtypeStruct((TOKENS, GLA_DV), BF16),
        scratch_shapes=[pltpu.VMEM((GLA_HEADS, GLA_DV_HEAD, GLA_DK_HEAD), F32)],
        compiler_params=_params(2),
        name="gla_scan",
    )(proj, proj, proj, proj, log_a, norm_g.reshape(1, GLA_DV_HEAD))


ROUTE_TM = 512
META_E0, META_E1, META_G0, META_G1, META_R0, META_R1 = range(6)


def _route_kernel(x_ref, g_ref, sh_ref, sc_ref, r_ref, h_ref, meta_ref, cnt_ref, run_ref):
    @pl.when(pl.program_id(0) == 0)
    def _():
        run_ref[...] = jnp.zeros(run_ref.shape, F32)

    h = _norm_mod(x_ref[...], g_ref[...], sc_ref[...], sh_ref[...])
    h_ref[...] = h
    h_hi = h.astype(BF16)
    h_lo = (h - h_hi.astype(F32)).astype(BF16)
    r = r_ref[...]
    r_hi = r.astype(BF16)
    r_lo = (r - r_hi.astype(F32)).astype(BF16)
    logits = _dot(h_hi, r_hi) + (_dot(h_hi, r_lo) + _dot(h_lo, r_hi))
    lane = lax.broadcasted_iota(jnp.int32, logits.shape, 1)
    logits = jnp.where(lane < N_EXPERTS, logits, -jnp.inf)
    m0 = jnp.max(logits, axis=1, keepdims=True)
    e0 = jnp.min(jnp.where(logits == m0, lane, LANES), axis=1, keepdims=True)
    rest = jnp.where(lane == e0, -jnp.inf, logits)
    m1 = jnp.max(rest, axis=1, keepdims=True)
    e1 = jnp.min(jnp.where(rest == m1, lane, LANES), axis=1, keepdims=True)
    p = jnp.exp(m1 - m0)
    gate0 = 1.0 / (1.0 + p)
    gate1 = p / (1.0 + p)

    hot0 = (lane == e0).astype(F32)
    hot1 = (lane == e1).astype(F32)
    both = hot0 + hot1
    row = lax.broadcasted_iota(jnp.int32, (ROUTE_TM, ROUTE_TM), 0)
    col = lax.broadcasted_iota(jnp.int32, (ROUTE_TM, ROUTE_TM), 1)
    before = _dot((col < row).astype(BF16), both.astype(BF16)) + run_ref[0:1, :]
    rank0 = jnp.sum(hot0 * before, axis=1, keepdims=True)
    rank1 = jnp.sum(hot1 * before, axis=1, keepdims=True)
    run_ref[...] = run_ref[...] + jnp.sum(both, axis=0, keepdims=True)
    cnt_ref[...] = run_ref[...]

    meta = jnp.zeros(logits.shape, F32)
    for lane_id, val in ((META_E0, e0.astype(F32)), (META_E1, e1.astype(F32)), (META_G0, gate0),
                         (META_G1, gate1), (META_R0, rank0), (META_R1, rank1)):
        meta = jnp.where(lane == lane_id, val, meta)
    meta_ref[...] = meta


def _route(x, g, sh, sc, router):
    rows, vec, seq_vec = _row_specs(ROUTE_TM)
    router = jnp.pad(router, ((0, 0), (0, LANES - N_EXPERTS)))
    return pl.pallas_call(
        _route_kernel,
        grid=(TOKENS // ROUTE_TM,),
        in_specs=[rows, vec, seq_vec, seq_vec, pl.BlockSpec((D_MODEL, LANES), lambda i: (0, 0))],
        out_specs=[
            rows,
            pl.BlockSpec((ROUTE_TM, LANES), lambda i: (i, 0)),
            pl.BlockSpec((SUBLANES, LANES), lambda i: (0, 0)),
        ],
        out_shape=[
            jax.ShapeDtypeStruct((TOKENS, D_MODEL), F32),
            jax.ShapeDtypeStruct((TOKENS, LANES), F32),
            jax.ShapeDtypeStruct((SUBLANES, LANES), F32),
        ],
        scratch_shapes=[pltpu.VMEM((SUBLANES, LANES), F32)],
        compiler_params=_params(1),
        name="moe_route",
    )(x, g.reshape(1, D_MODEL), sh, sc, router)


def _row_copy(src_hbm, src_row, dst_groups, group, sub, sem):
    return pltpu.make_async_copy(
        src_hbm.at[pl.ds(src_row, 1), :], dst_groups.at[group, pl.ds(sub, 1), :], sem)


def _for_rows(n_rows, fn):
    def body(group, carry):
        for sub in range(SUBLANES):
            fn(group, sub)
        return carry

    lax.fori_loop(0, n_rows // SUBLANES, body, 0)


GATHER_ROWS = 2 * MOE_ROWS
SCALAR_UNROLL = 32


def _gather_kernel(dest_ref, h_hbm, o_ref, tok_ref, buf_ref, sem):
    b = pl.program_id(0)

    @pl.when(b == 0)
    def _():
        def clear(s, carry):
            for u in range(SCALAR_UNROLL):
                tok_ref[s * SCALAR_UNROLL + u] = 0
            return carry

        def invert(t0, carry):
            for u in range(SCALAR_UNROLL // TOP_K):
                t = t0 * (SCALAR_UNROLL // TOP_K) + u
                for k in range(TOP_K):
                    tok_ref[dest_ref[TOP_K * t + k]] = t
            return carry

        lax.fori_loop(0, N_SLOTS // SCALAR_UNROLL, clear, 0)
        lax.fori_loop(0, TOKENS * TOP_K // SCALAR_UNROLL, invert, 0)

    def issue(tile):
        slot = tile & 1
        _for_rows(GATHER_ROWS, lambda group, sub: _row_copy(
            h_hbm, tok_ref[tile * GATHER_ROWS + group * SUBLANES + sub], buf_ref.at[slot], group, sub,
            sem.at[slot]).start())

    @pl.when(b == 0)
    def _():
        issue(b)

    @pl.when(b + 1 < pl.num_programs(0))
    def _():
        issue(b + 1)

    slot = b & 1
    _for_rows(GATHER_ROWS, lambda group, sub: _row_copy(
        h_hbm, 0, buf_ref.at[slot], group, sub, sem.at[slot]).wait())
    o_ref[...] = buf_ref[slot].reshape(GATHER_ROWS, D_MODEL).astype(o_ref.dtype)


def _gather_slots(dest_flat, h):
    return pl.pallas_call(
        _gather_kernel,
        grid_spec=pltpu.PrefetchScalarGridSpec(
            num_scalar_prefetch=1,
            grid=(N_SLOTS // GATHER_ROWS,),
            in_specs=[pl.BlockSpec(memory_space=pl.ANY)],
            out_specs=pl.BlockSpec((GATHER_ROWS, D_MODEL), lambda b, tok: (b, 0)),
            scratch_shapes=[pltpu.SMEM((N_SLOTS,), jnp.int32),
                            pltpu.VMEM((2, GATHER_ROWS // SUBLANES, SUBLANES, D_MODEL), F32),
                            pltpu.SemaphoreType.DMA((2,))],
        ),
        out_shape=jax.ShapeDtypeStruct((N_SLOTS, D_MODEL), BF16),
        compiler_params=_params(1),
        name="moe_gather",
    )(dest_flat, h)


BLOCK_DMA_PRIORITY = 1
WEIGHT_DMA_PRIORITY = 1
CHUNK_BLOCKS = 2


def _grouped_kernel(layer, col_offsets, compute, first_ref, cnt_ref, a_hbm, w_hbm, o_hbm,
                    wb_ref, w_buf, a_buf, o_buf, sem_w, sem_in, sem_out):
    f, e = pl.program_id(0), pl.program_id(1)
    n, first = cnt_ref[e], first_ref[e]
    col = pl.multiple_of(f * MM_TN, MM_TN)
    n_full = n // CHUNK_BLOCKS
    has_tail = n % CHUNK_BLOCKS == 1
    tail_blk = first + CHUNK_BLOCKS * n_full
    tail_slot = n_full & 1
    last_e = e == N_EXPERTS - 1
    is_first_step = jnp.logical_and(f == 0, e == 0)
    is_last_step = jnp.logical_and(last_e, f == pl.num_programs(0) - 1)
    e_next = jnp.where(last_e, 0, e + 1)
    f_next = jnp.where(last_e, f + 1, f)
    w_slot = (f * N_EXPERTS + e) & 1

    def weight_copies(tile, expert, slot):
        return [pltpu.make_async_copy(
            w_hbm.at[layer, expert, :, pl.ds(pl.multiple_of(tile * MM_TN + off, MM_TN), MM_TN)],
            w_buf.at[slot, k], sem_w.at[slot]) for k, off in enumerate(col_offsets)]

    def copy_in(blk, n_blocks, slot):
        rows = n_blocks * MOE_ROWS
        return pltpu.make_async_copy(
            a_hbm.at[pl.ds(blk * MOE_ROWS, rows), :], a_buf.at[slot, pl.ds(0, rows), :], sem_in.at[slot])

    def copy_out(blk, n_blocks, slot):
        rows = n_blocks * MOE_ROWS
        return pltpu.make_async_copy(
            o_buf.at[slot, pl.ds(0, rows), :],
            o_hbm.at[pl.ds(blk * MOE_ROWS, rows), pl.ds(col, MM_TN)], sem_out.at[slot])

    def start_first_chunk(expert):
        @pl.when(cnt_ref[expert] >= CHUNK_BLOCKS)
        def _():
            copy_in(first_ref[expert], CHUNK_BLOCKS, 0).start()

        @pl.when(cnt_ref[expert] == 1)
        def _():
            copy_in(first_ref[expert], 1, 0).start()

    @pl.when(is_first_step)
    def _():
        for cp in weight_copies(f, e, w_slot):
            cp.start()
        start_first_chunk(e)

    for cp in weight_copies(f, e, w_slot):
        cp.wait()
    for k in range(len(col_offsets)):
        wb_ref[k] = w_buf[w_slot, k].astype(BF16)

    @pl.when(jnp.logical_not(is_last_step))
    def _():
        for cp in weight_copies(f_next, e_next, 1 - w_slot):
            cp.start(priority=WEIGHT_DMA_PRIORITY)

    def full_chunk(c, carry):
        slot = c & 1
        blk = first + CHUNK_BLOCKS * c
        copy_in(blk, CHUNK_BLOCKS, slot).wait()

        @pl.when(c + 1 < n_full)
        def _():
            copy_in(blk + CHUNK_BLOCKS, CHUNK_BLOCKS, 1 - slot).start()

        @pl.when(jnp.logical_and(c + 1 == n_full, has_tail))
        def _():
            copy_in(blk + CHUNK_BLOCKS, 1, 1 - slot).start()

        @pl.when(c >= 2)
        def _():
            copy_out(blk - 2 * CHUNK_BLOCKS, CHUNK_BLOCKS, slot).wait()

        o_buf[slot] = compute(a_buf[slot], wb_ref).astype(o_buf.dtype)
        copy_out(blk, CHUNK_BLOCKS, slot).start()
        return carry

    lax.fori_loop(0, n_full, full_chunk, 0)

    @pl.when(has_tail)
    def _():
        copy_in(tail_blk, 1, tail_slot).wait()

        @pl.when(n_full >= 2)
        def _():
            copy_out(tail_blk - 2 * CHUNK_BLOCKS, CHUNK_BLOCKS, tail_slot).wait()

        o_buf[tail_slot, 0:MOE_ROWS, :] = compute(
            a_buf[tail_slot, 0:MOE_ROWS, :], wb_ref).astype(o_buf.dtype)
        copy_out(tail_blk, 1, tail_slot).start()

    @pl.when(jnp.logical_not(is_last_step))
    def _():
        start_first_chunk(e_next)

    @pl.when(jnp.logical_and(n_full >= 2, jnp.logical_not(has_tail)))
    def _():
        copy_out(tail_blk - 2 * CHUNK_BLOCKS, CHUNK_BLOCKS, tail_slot).wait()

    @pl.when(n_full >= 1)
    def _():
        copy_out(tail_blk - CHUNK_BLOCKS, CHUNK_BLOCKS, 1 - tail_slot).wait()

    @pl.when(has_tail)
    def _():
        copy_out(tail_blk, 1, tail_slot).wait()

    @pl.when(last_e)
    def _():
        o_buf[0, 0:MOE_ROWS, :] = jnp.zeros((MOE_ROWS, MM_TN), o_buf.dtype)

        def zero_block(blk, carry):
            copy_out(blk, 1, 0).start()
            copy_out(blk, 1, 0).wait()
            return carry

        lax.fori_loop(first + n, N_SLOT_BLOCKS, zero_block, 0)


def _grouped_matmul(name, compute, first_blk, n_blk, a, w4, layer, col_offsets, n_out, out_dtype):
    k = a.shape[1]
    n_w = len(col_offsets)
    return pl.pallas_call(
        functools.partial(_grouped_kernel, layer, col_offsets, compute),
        grid_spec=pltpu.PrefetchScalarGridSpec(
            num_scalar_prefetch=2,
            grid=(n_out // MM_TN, N_EXPERTS),
            in_specs=[pl.BlockSpec(memory_space=pl.ANY), pl.BlockSpec(memory_space=pl.ANY)],
            out_specs=pl.BlockSpec(memory_space=pl.ANY),
            scratch_shapes=[
                pltpu.VMEM((n_w, k, MM_TN), BF16),
                pltpu.VMEM((2, n_w, k, MM_TN), F32),
                pltpu.VMEM((2, CHUNK_BLOCKS * MOE_ROWS, k), BF16),
                pltpu.VMEM((2, CHUNK_BLOCKS * MOE_ROWS, MM_TN), out_dtype),
                pltpu.SemaphoreType.DMA((2,)),
                pltpu.SemaphoreType.DMA((2,)),
                pltpu.SemaphoreType.DMA((2,)),
            ],
        ),
        out_shape=jax.ShapeDtypeStruct((N_SLOTS, n_out), out_dtype),
        compiler_params=_params(2),
        name=name,
    )(first_blk, n_blk, a, w4)


def _swiglu_tile(a, wb_ref):
    return _silu(_dot(a, wb_ref[0])) * _dot(a, wb_ref[1])


def _down_tile(a, wb_ref):
    return _dot(a, wb_ref[0])


COMB_TM = 256


def _combine_kernel(final, dest_ref, x_ref, meta_ref, g2_ref, y_hbm, *refs):
    if final:
        fg_ref, o_ref, buf_ref, sem = refs
    else:
        ng_ref, nsh_ref, nsc_ref, o_ref, h_ref, buf_ref, sem = refs
    i = pl.program_id(0)

    def issue(tile):
        slot = tile & 1
        for k in range(TOP_K):
            _for_rows(COMB_TM, lambda group, sub: _row_copy(
                y_hbm, dest_ref[TOP_K * (tile * COMB_TM + group * SUBLANES + sub) + k],
                buf_ref.at[slot, k], group, sub, sem.at[slot]).start(priority=BLOCK_DMA_PRIORITY))

    @pl.when(i == 0)
    def _():
        issue(i)

    @pl.when(i + 1 < pl.num_programs(0))
    def _():
        issue(i + 1)

    slot = i & 1
    for k in range(TOP_K):
        _for_rows(COMB_TM, lambda group, sub: _row_copy(
            y_hbm, 0, buf_ref.at[slot, k], group, sub, sem.at[slot]).wait())
    meta = meta_ref[...]
    y0 = buf_ref[slot, 0].reshape(COMB_TM, D_MODEL)
    y1 = buf_ref[slot, 1].reshape(COMB_TM, D_MODEL)
    f = meta[:, META_G0:META_G0 + 1] * y0 + meta[:, META_G1:META_G1 + 1] * y1
    x_new = x_ref[...] + g2_ref[...] * f
    if final:
        ms = jnp.mean(x_new * x_new, axis=-1, keepdims=True)
        o_ref[...] = x_new * lax.rsqrt(ms + EPS) * fg_ref[...]
    else:
        o_ref[...] = x_new
        h_ref[...] = _norm_mod(x_new, ng_ref[...], nsc_ref[...], nsh_ref[...]).astype(h_ref.dtype)


def _combine(dest_flat, x, meta, g2, y_slots, final_g=None, next_norm=None):
    final = next_norm is None
    per_seq = SEQ // COMB_TM
    rows = pl.BlockSpec((COMB_TM, D_MODEL), lambda i, d: (i, 0))
    vec = pl.BlockSpec((1, D_MODEL), lambda i, d: (0, 0))
    seq_vec = pl.BlockSpec((None, 1, D_MODEL), lambda i, d: (i // per_seq, 0, 0))
    if final:
        extra_in, extra_specs = (final_g.reshape(1, D_MODEL),), [vec]
        out_specs, out_shape = rows, jax.ShapeDtypeStruct((TOKENS, D_MODEL), F32)
    else:
        g, sh, sc = next_norm
        extra_in, extra_specs = (g.reshape(1, D_MODEL), sh, sc), [vec, seq_vec, seq_vec]
        out_specs = [rows, rows]
        out_shape = [jax.ShapeDtypeStruct((TOKENS, D_MODEL), F32),
                     jax.ShapeDtypeStruct((TOKENS, D_MODEL), BF16)]
    return pl.pallas_call(
        functools.partial(_combine_kernel, final),
        grid_spec=pltpu.PrefetchScalarGridSpec(
            num_scalar_prefetch=1,
            grid=(TOKENS // COMB_TM,),
            in_specs=[rows, pl.BlockSpec((COMB_TM, LANES), lambda i, d: (i, 0)), seq_vec,
                      pl.BlockSpec(memory_space=pl.ANY)] + extra_specs,
            out_specs=out_specs,
            scratch_shapes=[pltpu.VMEM((2, TOP_K, COMB_TM // SUBLANES, SUBLANES, D_MODEL), F32),
                            pltpu.SemaphoreType.DMA((2,))],
        ),
        out_shape=out_shape,
        compiler_params=_params(1),
        name="moe_combine",
    )(dest_flat, x, meta, g2, y_slots, *extra_in)


def _slot_plan(meta, counts):
    top_e = meta[:, META_E0:META_E1 + 1].astype(jnp.int32)
    rank = meta[:, META_R0:META_R1 + 1].astype(jnp.int32)
    counts = counts[0, :N_EXPERTS].astype(jnp.int32)
    n_blk = (counts + MOE_ROWS - 1) // MOE_ROWS
    first_blk = jnp.cumsum(n_blk) - n_blk
    dest = (first_blk[top_e] * MOE_ROWS + rank).reshape(TOKENS * TOP_K)
    return dest, first_blk.astype(jnp.int32), n_blk.astype(jnp.int32)


def kernel(x, c, ada_w, ada_b, norm_g, conv_w_in, conv_k, conv_w_out, gla_w_in, gla_w_gk, gla_b_gk,
           gla_norm_g, gla_w_out, ffn_w13, ffn_w2, moe_router, moe_w13, moe_w2, final_g):
    assert x.shape == (BATCH, SEQ, D_MODEL) and x.dtype == F32
    assert DEPTH % 2 == 0
    mod = _ada_all(c, ada_w, ada_b)[:, :BATCH]
    xt = x.reshape(TOKENS, D_MODEL)
    gla_w_in_t = jnp.swapaxes(gla_w_in, 1, 2)
    mods = [[mod[i, :, n * D_MODEL:(n + 1) * D_MODEL].reshape(BATCH, 1, D_MODEL) for n in range(6)]
            for i in range(DEPTH)]
    h = None
    for i in range(DEPTH):
        j = i // 2
        sh1, sc1, g1, sh2, sc2, g2 = mods[i]
        if h is None:
            h = _norm_modulate(xt, norm_g[i, 0], sh1, sc1)
        if i % 2 == 0:
            y = _conv_in(h, conv_w_in, conv_k, j)
            xt = _matmul_residual(y, conv_w_out, j, xt, g1)
            h = _norm_modulate(xt, norm_g[i, 1], sh2, sc2)
            t = _ffn_up(h, ffn_w13, j)
            xt = _matmul_residual(t, ffn_w2, j, xt, g2)
            h = None
        else:
            proj = _matmul_nt(h, gla_w_in_t, j, GLA_MAIN, BF16)
            log_a = _gla_log_decay(h, gla_w_in_t, j, gla_w_gk[j], gla_b_gk[j])
            o = _gla_scan(proj, log_a, gla_norm_g[j])
            xt = _matmul_residual(o, gla_w_out, j, xt, g1)
            h32, meta, counts = _route(xt, norm_g[i, 1], sh2, sc2, moe_router[j])
            dest, first_blk, n_blk = _slot_plan(meta, counts)
            xs = _gather_slots(dest, h32)
            t = _grouped_matmul("moe_up", _swiglu_tile, first_blk, n_blk, xs, moe_w13, j,
                                (0, D_FF), D_FF, BF16)
            y = _grouped_matmul("moe_down", _down_tile, first_blk, n_blk, t, moe_w2, j,
                                (0,), D_MODEL, F32)
            if i == DEPTH - 1:
                xt = _combine(dest, xt, meta, g2, y, final_g=final_g)
            else:
                nsh, nsc = mods[i + 1][0], mods[i + 1][1]
                xt, h = _combine(dest, xt, meta, g2, y, next_norm=(norm_g[i + 1, 0], nsh, nsc))
    return xt.reshape(BATCH, SEQ, D_MODEL)
```
